```python
import jax, jax.numpy as jnp
from jax import lax
import numpy as np

D_MODEL = 1024
BATCH = 32
SEQ = 2048
DEPTH = 4

HEAD_DIM = 64
MIX_W = D_MODEL
A_HEADS = MIX_W // 2 // HEAD_DIM
A_KV_HEADS = A_HEADS // 4
B_HEADS = MIX_W // 2 // HEAD_DIM
C_WIDTH = MIX_W // 2
D_WIDTH = MIX_W // 2
C_CONV = 3
D_CONV = 31
D_FF = 2816
GRID_W = 64
NA_ROWS = 8
NA_COLS = 16
Q_BLOCK = 128
ROPE_THETA = 10000.0
EPS = 1e-6
NEG_INF = -1e30
N_EVEN = (DEPTH + 1) // 2
N_ODD = DEPTH // 2
A_Q = A_HEADS * HEAD_DIM
A_KV = A_KV_HEADS * HEAD_DIM
B_W = B_HEADS * HEAD_DIM
AB_IN = A_Q + 2 * A_KV + 3 * B_W
AB_OUT = A_Q + B_W
CD_IN = 3 * C_WIDTH + 2 * D_WIDTH
CD_OUT = C_WIDTH + D_WIDTH

kernel_name = "hybrid_gqa_natten_shortconv_conformer_encoder"


def rms_norm(x, g):
    xf = x.astype(jnp.float32)
    y = xf * lax.rsqrt(jnp.mean(xf * xf, axis=-1, keepdims=True) + EPS)
    return (y * g.astype(jnp.float32)).astype(x.dtype)


def layer_norm(x, g, b):
    xf = x.astype(jnp.float32)
    mu = jnp.mean(xf, axis=-1, keepdims=True)
    var = jnp.mean(jnp.square(xf - mu), axis=-1, keepdims=True)
    y = (xf - mu) * lax.rsqrt(var + EPS)
    return (y * g.astype(jnp.float32) + b.astype(jnp.float32)).astype(x.dtype)


def swiglu(x, w_gate, w_up, w_down):
    return (jax.nn.silu(x @ w_gate) * (x @ w_up)) @ w_down


def axial_rope(seq):
    t = jnp.arange(seq)
    row = (t // GRID_W).astype(jnp.float32)
    col = (t % GRID_W).astype(jnp.float32)
    half = HEAD_DIM // 2
    freqs = ROPE_THETA ** (-jnp.arange(0, half, 2, dtype=jnp.float32) / half)
    ang = jnp.concatenate([row[:, None] * freqs, col[:, None] * freqs], axis=-1)
    return jnp.cos(ang), jnp.sin(ang)


def apply_rope(x, cos, sin):
    xf = x.astype(jnp.float32).reshape(*x.shape[:-1], x.shape[-1] // 2, 2)
    x0, x1 = xf[..., 0], xf[..., 1]
    c = cos[None, :, None, :]
    s = sin[None, :, None, :]
    out = jnp.stack([x0 * c - x1 * s, x0 * s + x1 * c], axis=-1)
    return out.reshape(x.shape).astype(x.dtype)


def depthwise_conv(u, w):
    k = w.shape[0]
    return lax.conv_general_dilated(
        u, w[:, None, :].astype(u.dtype), window_strides=(1,),
        padding=[(k // 2, k // 2)], dimension_numbers=("NWC", "WIO", "NWC"),
        feature_group_count=u.shape[-1])


def global_gqa(q, k, v):
    b, s, ha, d = q.shape
    kv = k.shape[2]
    g = ha // kv
    nqb = s // Q_BLOCK
    scale = d ** -0.5
    qb = q.reshape(b, nqb, Q_BLOCK, kv, g, d).transpose(1, 0, 2, 3, 4, 5)

    def block(qi):
        sc = jnp.einsum("bqkgd,bskd->bkgqs", qi, k).astype(jnp.float32) * scale
        p = jax.nn.softmax(sc, axis=-1).astype(v.dtype)
        return jnp.einsum("bkgqs,bskd->bqkgd", p, v)

    o = lax.map(block, qb)
    return o.transpose(1, 0, 2, 3, 4, 5).reshape(b, s, ha * d)


def neighborhood_attention(q, k, v, rpb):
    b, s, h, d = q.shape
    rows = s // GRID_W
    wr = min(NA_ROWS, rows)
    wc = NA_COLS
    kw = 2 * wc
    ncb = GRID_W // wc
    scale = d ** -0.5
    qg = q.reshape(b, rows, GRID_W, h, d)
    kg = k.reshape(b, rows, GRID_W, h, d)
    vg = v.reshape(b, rows, GRID_W, h, d)
    qcol = jnp.arange(GRID_W).reshape(ncb, wc)
    kcol = jnp.clip(jnp.arange(ncb) * wc - wc // 2, 0, GRID_W - kw)[:, None] + jnp.arange(kw)
    wcs = jnp.clip(qcol - wc // 2, 0, GRID_W - wc)
    kc = kcol[:, None, :]
    col_mask = (kc >= wcs[..., None]) & (kc < wcs[..., None] + wc)
    col_off = jnp.clip(kc - qcol[..., None], -(NA_COLS - 1), NA_COLS - 1) + NA_COLS - 1
    rpb_f = rpb.astype(jnp.float32)

    def row_block(r):
        rs = jnp.clip(r - wr // 2, 0, rows - wr)
        kb = lax.dynamic_slice_in_dim(kg, rs, wr, axis=1)[:, :, kcol]
        vb = lax.dynamic_slice_in_dim(vg, rs, wr, axis=1)[:, :, kcol]
        qr = lax.dynamic_index_in_dim(qg, r, axis=1, keepdims=False).reshape(b, ncb, wc, h, d)
        sc = jnp.einsum("bnqhd,bwnkhd->bhnqwk", qr, kb).astype(jnp.float32) * scale
        row_off = rs + jnp.arange(wr) - r + NA_ROWS - 1
        bias = rpb_f[:, row_off[None, None, :, None], col_off[:, :, None, :]]
        sc = jnp.where(col_mask[:, :, None, :], sc + bias, NEG_INF)
        p = jax.nn.softmax(sc.reshape(b, h, ncb, wc, wr * kw), axis=-1)
        p = p.reshape(sc.shape).astype(v.dtype)
        return jnp.einsum("bhnqwk,bwnkhd->bnqhd", p, vb).reshape(b, GRID_W, h, d)

    o = lax.map(row_block, jnp.arange(rows))
    return o.transpose(1, 0, 2, 3, 4).reshape(b, s, h * d)


def mixer_ab(hx, w_in, w_out, q_norm, k_norm, rpb, cos, sin):
    b, s, _ = hx.shape
    u = hx @ w_in
    aq, ak, av, bq, bk, bv = jnp.split(
        u, [A_Q, A_Q + A_KV, A_Q + 2 * A_KV, A_Q + 2 * A_KV + B_W, A_Q + 2 * A_KV + 2 * B_W], axis=-1)
    aq = apply_rope(rms_norm(aq.reshape(b, s, A_HEADS, HEAD_DIM), q_norm), cos, sin)
    ak = apply_rope(rms_norm(ak.reshape(b, s, A_KV_HEADS, HEAD_DIM), k_norm), cos, sin)
    ya = global_gqa(aq, ak, av.reshape(b, s, A_KV_HEADS, HEAD_DIM))
    yb = neighborhood_attention(bq.reshape(b, s, B_HEADS, HEAD_DIM),
                                bk.reshape(b, s, B_HEADS, HEAD_DIM),
                                bv.reshape(b, s, B_HEADS, HEAD_DIM), rpb)
    return jnp.concatenate([ya, yb], axis=-1) @ w_out


def mixer_cd(hx, w_in, w_out, c_conv_w, d_conv_w, d_norm_g, d_norm_b):
    u = hx @ w_in
    c_h, c_b, c_c, d_a, d_g = jnp.split(
        u, [C_WIDTH, 2 * C_WIDTH, 3 * C_WIDTH, 3 * C_WIDTH + D_WIDTH], axis=-1)
    yc = c_b * depthwise_conv(c_c * c_h, c_conv_w)
    yd = depthwise_conv(d_a * jax.nn.sigmoid(d_g), d_conv_w)
    yd = jax.nn.silu(layer_norm(yd, d_norm_g, d_norm_b))
    return jnp.concatenate([yc, yd], axis=-1) @ w_out


def setup_inputs(seed: int = 0) -> dict:
    key = jax.random.key(seed)
    ks = jax.random.split(key, 20)
    f32 = jnp.float32

    def nrm(k, shape, fan_in):
        return jax.random.normal(k, shape, f32) * (fan_in ** -0.5)

    def gain(k, shape):
        return 1.0 + 0.02 * jax.random.normal(k, shape, f32)

    return {
        "x": jax.random.normal(ks[0], (BATCH, SEQ, D_MODEL), f32),
        "ffn_norm": gain(ks[1], (DEPTH, 2, D_MODEL)),
        "mix_norm": gain(ks[2], (DEPTH, D_MODEL)),
        "ffn_w_gate": nrm(ks[3], (DEPTH, 2, D_MODEL, D_FF), D_MODEL),
        "ffn_w_up": nrm(ks[4], (DEPTH, 2, D_MODEL, D_FF), D_MODEL),
        "ffn_w_down": nrm(ks[5], (DEPTH, 2, D_FF, D_MODEL), D_FF),
        "ab_w_in": nrm(ks[6], (N_EVEN, D_MODEL, AB_IN), D_MODEL),
        "ab_w_out": nrm(ks[7], (N_EVEN, AB_OUT, D_MODEL), AB_OUT),
        "a_q_norm": gain(ks[8], (N_EVEN, HEAD_DIM)),
        "a_k_norm": gain(ks[9], (N_EVEN, HEAD_DIM)),
        "b_rpb": 0.1 * jax.random.normal(ks[10], (N_EVEN, B_HEADS, 2 * NA_ROWS - 1, 2 * NA_COLS - 1), f32),
        "cd_w_in": nrm(ks[11], (N_ODD, D_MODEL, CD_IN), D_MODEL),
        "cd_w_out": nrm(ks[12], (N_ODD, CD_OUT, D_MODEL), CD_OUT),
        "c_conv_w": nrm(ks[13], (N_ODD, C_CONV, C_WIDTH), C_CONV),
        "d_conv_w": nrm(ks[14], (N_ODD, D_CONV, D_WIDTH), D_CONV),
        "d_norm_g": gain(ks[15], (N_ODD, D_WIDTH)),
        "d_norm_b": 0.02 * jax.random.normal(ks[16], (N_ODD, D_WIDTH), f32),
        "final_norm": gain(ks[17], (D_MODEL,)),
    }


def reference(x, ffn_norm, mix_norm, ffn_w_gate, ffn_w_up, ffn_w_down, ab_w_in, ab_w_out,
              a_q_norm, a_k_norm, b_rpb, cd_w_in, cd_w_out, c_conv_w, d_conv_w,
              d_norm_g, d_norm_b, final_norm):
    cos, sin = axial_rope(x.shape[1])
    for i in range(DEPTH):
        x = x + 0.5 * swiglu(rms_norm(x, ffn_norm[i, 0]), ffn_w_gate[i, 0], ffn_w_up[i, 0], ffn_w_down[i, 0])
        hx = rms_norm(x, mix_norm[i])
        j = i // 2
        if i % 2 == 0:
            x = x + mixer_ab(hx, ab_w_in[j], ab_w_out[j], a_q_norm[j], a_k_norm[j], b_rpb[j], cos, sin)
        else:
            x = x + mixer_cd(hx, cd_w_in[j], cd_w_out[j], c_conv_w[j], d_conv_w[j], d_norm_g[j], d_norm_b[j])
        x = x + 0.5 * swiglu(rms_norm(x, ffn_norm[i, 1]), ffn_w_gate[i, 1], ffn_w_up[i, 1], ffn_w_down[i, 1])
    return rms_norm(x, final_norm)
```

```python
import functools

import jax
import jax.numpy as jnp
import numpy as np
from jax import lax
from jax.experimental import pallas as pl
from jax.experimental.pallas import tpu as pltpu

D_MODEL = 1024
DEPTH = 4
HEAD_DIM = 64
A_HEADS = 8
A_KV_HEADS = 2
B_HEADS = 8
C_WIDTH = 512
D_WIDTH = 512
C_CONV = 3
D_CONV = 31
D_FF = 2816
GRID_W = 64
NA_ROWS = 8
NA_COLS = 16
ROPE_THETA = 10000.0
EPS = 1e-6
NEG_INF = -1e30
A_Q = A_HEADS * HEAD_DIM
A_KV = A_KV_HEADS * HEAD_DIM
B_W = B_HEADS * HEAD_DIM

LANES = 128
VMEM_LIMIT_BYTES = 56 * 1024 * 1024
TOKEN_TILE = 512
FF_CHUNK = 256
Q_TILE = 256
CONV_ROWS = 128
D_HALO = 16

F32 = jnp.float32
BF16 = jnp.bfloat16


def _params(*sem):
    return pltpu.CompilerParams(dimension_semantics=sem, vmem_limit_bytes=VMEM_LIMIT_BYTES)


def _resident(shape):
    zeros = (0,) * len(shape)
    return pl.BlockSpec(shape, lambda *_: zeros, pipeline_mode=pl.Buffered(1))


def _rms_rows(x, gain):
    ms = jnp.mean(x * x, axis=-1, keepdims=True)
    return x * lax.rsqrt(ms + EPS) * gain


def _ffn_kernel(x_ref, nw_ref, wg_ref, wu_ref, wd_ref, o_ref, h_ref, acc_ref):
    h_ref[...] = _rms_rows(x_ref[...], nw_ref[...]).astype(BF16)
    acc_ref[...] = jnp.zeros_like(acc_ref)

    def chunk(c, carry):
        h = h_ref[...]
        g = jnp.dot(h, wg_ref[c], preferred_element_type=F32)
        u = jnp.dot(h, wu_ref[c], preferred_element_type=F32)
        a = (g * jax.nn.sigmoid(g) * u).astype(BF16)
        acc_ref[...] += jnp.dot(a, wd_ref[c], preferred_element_type=F32)
        return carry

    lax.fori_loop(0, D_FF // FF_CHUNK, chunk, 0)
    o_ref[...] = x_ref[...] + 0.5 * acc_ref[...]


def _ffn(x, nw, wg, wu, wd):
    n = x.shape[0]
    nch = D_FF // FF_CHUNK
    tile = pl.BlockSpec((TOKEN_TILE, D_MODEL), lambda i: (i, 0))
    return pl.pallas_call(
        _ffn_kernel,
        grid=(n // TOKEN_TILE,),
        in_specs=[tile, _resident((1, D_MODEL)),
                  _resident((nch, D_MODEL, FF_CHUNK)), _resident((nch, D_MODEL, FF_CHUNK)),
                  _resident((nch, FF_CHUNK, D_MODEL))],
        out_specs=tile,
        out_shape=jax.ShapeDtypeStruct(x.shape, F32),
        scratch_shapes=[pltpu.VMEM((TOKEN_TILE, D_MODEL), BF16),
                        pltpu.VMEM((TOKEN_TILE, D_MODEL), F32)],
        compiler_params=_params("parallel"),
        name="ffn",
    )(x, nw, wg, wu, wd)


def _swap_pairs(x):
    lane = lax.broadcasted_iota(jnp.int32, x.shape, 1)
    return jnp.where(lane % 2 == 0, pltpu.roll(x, LANES - 1, 1), pltpu.roll(x, 1, 1))


def _head_norm_rope(xg, gain, cos, sin_signed, seg_ones):
    sq = xg * xg
    hi = sq.astype(BF16)
    lo = (sq - hi.astype(F32)).astype(BF16)
    ssum = (jnp.dot(hi, seg_ones, preferred_element_type=F32)
            + jnp.dot(lo, seg_ones, preferred_element_type=F32))
    xn = xg * lax.rsqrt(ssum * (1.0 / HEAD_DIM) + EPS) * gain
    return xn * cos + _swap_pairs(xn) * sin_signed


def _dup_halves(x):
    lane = lax.broadcasted_iota(jnp.int32, x.shape, 1)
    rolled = pltpu.roll(x, HEAD_DIM, 1)
    low = lane < HEAD_DIM
    return jnp.where(low, x, rolled), jnp.where(low, rolled, x)


def _ab_in_kernel(x_ref, nw_ref, w_ref, qg_ref, kg_ref, cos_ref, sin_ref, seg_ref,
                  aq_ref, ak_ref, av_ref, bq_ref, bk_ref, bv_ref, h_ref):
    h_ref[...] = _rms_rows(x_ref[...], nw_ref[...]).astype(BF16)
    h = h_ref[...]
    cos = cos_ref[...]
    sin = sin_ref[...]
    seg = seg_ref[...]
    scale = HEAD_DIM ** -0.5

    def proj(lo, width):
        return jnp.dot(h, w_ref[:, lo:lo + width], preferred_element_type=F32)

    for j in range(A_Q // LANES):
        q = _head_norm_rope(proj(j * LANES, LANES), qg_ref[...], cos, sin, seg)
        aq_ref[:, j * LANES:(j + 1) * LANES] = (q * scale).astype(BF16)
    k = _head_norm_rope(proj(A_Q, A_KV), kg_ref[...], cos, sin, seg)
    k0, k1 = _dup_halves(k)
    ak_ref[:, :LANES] = k0.astype(BF16)
    ak_ref[:, LANES:] = k1.astype(BF16)
    v0, v1 = _dup_halves(proj(A_Q + A_KV, A_KV))
    av_ref[:, :LANES] = v0.astype(BF16)
    av_ref[:, LANES:] = v1.astype(BF16)
    base = A_Q + 2 * A_KV
    bq_ref[...] = (proj(base, B_W) * scale).astype(BF16)
    bk_ref[...] = proj(base + B_W, B_W).astype(BF16)
    bv_ref[...] = proj(base + 2 * B_W, B_W).astype(BF16)


def _ab_in(x, nw, w_in, qg, kg, cos_t, sin_t, seg, seq):
    n = x.shape[0]
    ab_in = w_in.shape[1]
    tiles_per_seq = seq // TOKEN_TILE
    tile = lambda w: pl.BlockSpec((TOKEN_TILE, w), lambda i: (i, 0))
    rope = pl.BlockSpec((TOKEN_TILE, LANES), lambda i: (i % tiles_per_seq, 0))
    widths = (A_Q, 2 * LANES, 2 * LANES, B_W, B_W, B_W)
    return pl.pallas_call(
        _ab_in_kernel,
        grid=(n // TOKEN_TILE,),
        in_specs=[tile(D_MODEL), _resident((1, D_MODEL)), _resident((D_MODEL, ab_in)),
                  _resident((1, LANES)), _resident((1, LANES)), rope, rope,
                  _resident((LANES, LANES))],
        out_specs=[tile(w) for w in widths],
        out_shape=[jax.ShapeDtypeStruct((n, w), BF16) for w in widths],
        scratch_shapes=[pltpu.VMEM((TOKEN_TILE, D_MODEL), BF16)],
        compiler_params=_params("parallel"),
        name="ab_in",
    )(x, nw, w_in, qg, kg, cos_t, sin_t, seg)


def _stack_heads(qp):
    lane = lax.broadcasted_iota(jnp.int32, qp.shape, 1)
    zero = jnp.zeros_like(qp)
    return jnp.concatenate([jnp.where(lane < HEAD_DIM, qp, zero),
                            jnp.where(lane < HEAD_DIM, zero, qp)], axis=0)


def _unstack_heads(o, rows):
    lane = lax.broadcasted_iota(jnp.int32, (rows, LANES), 1)
    return jnp.where(lane < HEAD_DIM, o[:rows], o[rows:])


def _softmax_pv(s, v):
    m = jnp.max(s, axis=-1, keepdims=True)
    p = jnp.exp(s - m)
    l = jnp.sum(p, axis=-1, keepdims=True)
    return jnp.dot(p.astype(BF16), v, preferred_element_type=F32) / l


_NT = (((1,), (1,)), ((), ()))


def _gqa_kernel(q_ref, k_ref, v_ref, o_ref):
    k = k_ref[0]
    v = v_ref[0]
    for j in range(2):
        qs = _stack_heads(q_ref[0, :, j * LANES:(j + 1) * LANES])
        s = lax.dot_general(qs, k, _NT, preferred_element_type=F32)
        o = _softmax_pv(s, v)
        o_ref[0, :, j * LANES:(j + 1) * LANES] = _unstack_heads(o, Q_TILE).astype(BF16)


def _gqa(aq, akd, avd):
    b, s, _ = aq.shape
    qspec = pl.BlockSpec((1, Q_TILE, 2 * LANES), lambda bi, g, qi: (bi, qi, g))
    kvspec = pl.BlockSpec((1, s, LANES), lambda bi, g, qi: (bi, 0, g))
    return pl.pallas_call(
        _gqa_kernel,
        grid=(b, A_KV_HEADS, s // Q_TILE),
        in_specs=[qspec, kvspec, kvspec],
        out_specs=qspec,
        out_shape=jax.ShapeDtypeStruct(aq.shape, BF16),
        compiler_params=_params("parallel", "parallel", "arbitrary"),
        name="gqa",
    )(aq, akd, avd)


def _na_kernel(q_ref, k_ref, v_ref, bias_ref, o_ref, *, rows):
    band = NA_ROWS * GRID_W

    def row(r, carry):
        rs = jnp.clip(r - NA_ROWS // 2, 0, rows - NA_ROWS)
        d = r - rs
        q0 = pl.multiple_of(r * GRID_W, GRID_W)
        k0 = pl.multiple_of(rs * GRID_W, GRID_W)
        for j in range(B_W // LANES):
            cols = slice(j * LANES, (j + 1) * LANES)
            qs = _stack_heads(q_ref[0, pl.ds(q0, GRID_W), cols])
            kb = k_ref[0, pl.ds(k0, band), cols]
            vb = v_ref[0, pl.ds(k0, band), cols]
            s = lax.dot_general(qs, kb, _NT, preferred_element_type=F32) + bias_ref[d, j]
            o = _softmax_pv(s, vb)
            o_ref[0, pl.ds(q0, GRID_W), cols] = _unstack_heads(o, GRID_W).astype(BF16)
        return carry

    lax.fori_loop(0, rows, row, 0)


def _na(bq, bk, bv, bias):
    b, s, w = bq.shape
    rows = s // GRID_W
    spec = pl.BlockSpec((1, s, w), lambda bi: (bi, 0, 0))
    return pl.pallas_call(
        functools.partial(_na_kernel, rows=rows),
        grid=(b,),
        in_specs=[spec, spec, spec, _resident(bias.shape)],
        out_specs=spec,
        out_shape=jax.ShapeDtypeStruct(bq.shape, BF16),
        compiler_params=_params("parallel"),
        name="na",
    )(bq, bk, bv, bias)


def _na_bias(rpb, rows):
    wr = min(NA_ROWS, rows)
    qc = np.arange(GRID_W)[:, None]
    kc = np.arange(GRID_W)[None, :]
    col_off = np.clip(kc - qc, -(NA_COLS - 1), NA_COLS - 1) + NA_COLS - 1
    wcs = np.clip(qc - NA_COLS // 2, 0, GRID_W - NA_COLS)
    valid = (kc >= wcs) & (kc < wcs + NA_COLS)
    d = np.arange(wr)[:, None]
    w = np.arange(wr)[None, :]
    row_off = w - d + NA_ROWS - 1
    t = rpb.astype(F32)[:, row_off[:, :, None, None], col_off[None, None]]
    t = jnp.where(valid[None, None, None], t, NEG_INF)
    t = t.transpose(1, 0, 3, 2, 4)
    return t.reshape(wr, B_HEADS // 2, 2 * GRID_W, wr * GRID_W)


def _out_kernel(x_ref, ya_ref, yb_ref, w_ref, o_ref):
    half = ya_ref.shape[1]
    o_ref[...] = (x_ref[...]
                  + jnp.dot(ya_ref[...], w_ref[:half], preferred_element_type=F32)
                  + jnp.dot(yb_ref[...], w_ref[half:], preferred_element_type=F32))


def _mix_out(x, ya, yb, w_out):
    n = x.shape[0]
    tile = lambda w: pl.BlockSpec((TOKEN_TILE, w), lambda i: (i, 0))
    return pl.pallas_call(
        _out_kernel,
        grid=(n // TOKEN_TILE,),
        in_specs=[tile(D_MODEL), tile(ya.shape[1]), tile(yb.shape[1]), _resident(w_out.shape)],
        out_specs=tile(D_MODEL),
        out_shape=jax.ShapeDtypeStruct(x.shape, F32),
        compiler_params=_params("parallel"),
        name="mix_out",
    )(x, ya, yb, w_out)


def _cd_in_kernel(x_ref, nw_ref, w_ref, zc_ref, cb_ref, zd_ref, h_ref):
    h_ref[...] = _rms_rows(x_ref[...], nw_ref[...]).astype(BF16)
    h = h_ref[...]

    def proj(i):
        return jnp.dot(h, w_ref[:, i * C_WIDTH:(i + 1) * C_WIDTH], preferred_element_type=F32)

    zc_ref[...] = (proj(2) * proj(0)).astype(BF16)
    cb_ref[...] = proj(1).astype(BF16)
    zd_ref[...] = (proj(3) * jax.nn.sigmoid(proj(4))).astype(BF16)


def _cd_in(x, nw, w_in):
    n = x.shape[0]
    tile = lambda w: pl.BlockSpec((TOKEN_TILE, w), lambda i: (i, 0))
    return pl.pallas_call(
        _cd_in_kernel,
        grid=(n // TOKEN_TILE,),
        in_specs=[tile(D_MODEL), _resident((1, D_MODEL)), _resident(w_in.shape)],
        out_specs=[tile(C_WIDTH)] * 3,
        out_shape=[jax.ShapeDtypeStruct((n, C_WIDTH), BF16)] * 3,
        scratch_shapes=[pltpu.VMEM((TOKEN_TILE, D_MODEL), BF16)],
        compiler_params=_params("parallel"),
        name="cd_in",
    )(x, nw, w_in)


def _conv_kernel(zc_ref, cb_ref, zd_ref, wc_ref, wd_ref, g_ref, b_ref, yc_ref, yd_ref,
                 pad_ref, shift_ref, raw_ref, *, seq):
    padded = seq + 2 * D_HALO
    span = padded - 8
    halo = jnp.zeros((D_HALO, C_WIDTH), F32)
    nchunks = seq // CONV_ROWS

    def conv(src_ref, w_ref, taps, emit):
        pad_ref[:D_HALO] = halo
        pad_ref[D_HALO + seq:] = halo
        pad_ref[D_HALO:D_HALO + seq] = src_ref[0].astype(F32)
        first = D_HALO - taps // 2
        for c in range(C_WIDTH // LANES):
            cols = slice(c * LANES, (c + 1) * LANES)
            for s in range(8):
                shift_ref[s] = pad_ref[s:s + span, cols]
            w = w_ref[:, cols]

            def chunk(i, carry):
                r0 = pl.multiple_of(i * CONV_ROWS, CONV_ROWS)
                acc = jnp.zeros((CONV_ROWS, LANES), F32)
                for k in range(taps):
                    off = first + k
                    src = shift_ref[off % 8, pl.ds(r0 + (off // 8) * 8, CONV_ROWS), :]
                    acc = acc + w[k:k + 1, :] * src
                emit(r0, cols, acc)
                return carry

            lax.fori_loop(0, nchunks, chunk, 0)

    def emit_c(r0, cols, acc):
        yc_ref[0, pl.ds(r0, CONV_ROWS), cols] = (
            cb_ref[0, pl.ds(r0, CONV_ROWS), cols].astype(F32) * acc).astype(BF16)

    def emit_d(r0, cols, acc):
        raw_ref[pl.ds(r0, CONV_ROWS), cols] = acc

    conv(zc_ref, wc_ref, C_CONV, emit_c)
    conv(zd_ref, wd_ref, D_CONV, emit_d)

    def norm(i, carry):
        r0 = pl.multiple_of(i * CONV_ROWS, CONV_ROWS)
        y = raw_ref[pl.ds(r0, CONV_ROWS), :]
        mu = jnp.mean(y, axis=-1, keepdims=True)
        var = jnp.mean(jnp.square(y - mu), axis=-1, keepdims=True)
        z = (y - mu) * lax.rsqrt(var + EPS) * g_ref[...] + b_ref[...]
        yd_ref[0, pl.ds(r0, CONV_ROWS), :] = (z * jax.nn.sigmoid(z)).astype(BF16)
        return carry

    lax.fori_loop(0, nchunks, norm, 0)


def _conv(zc, cb, zd, wc, wd, g, b):
    bsz, seq, w = zc.shape
    padded = seq + 2 * D_HALO
    spec = pl.BlockSpec((1, seq, w), lambda bi: (bi, 0, 0))
    return pl.pallas_call(
        functools.partial(_conv_kernel, seq=seq),
        grid=(bsz,),
        in_specs=[spec, spec, spec, _resident(wc.shape), _resident(wd.shape),
                  _resident((1, w)), _resident((1, w))],
        out_specs=[spec, spec],
        out_shape=[jax.ShapeDtypeStruct(zc.shape, BF16)] * 2,
        scratch_shapes=[pltpu.VMEM((padded, w), F32),
                        pltpu.VMEM((8, padded - 8, LANES), F32),
                        pltpu.VMEM((seq, w), F32)],
        compiler_params=_params("parallel"),
        name="conv",
    )(zc, cb, zd, wc, wd, g, b)


def _final_kernel(x_ref, nw_ref, o_ref):
    o_ref[...] = _rms_rows(x_ref[...], nw_ref[...])


def _final_norm(x, nw):
    n = x.shape[0]
    tile = pl.BlockSpec((TOKEN_TILE, D_MODEL), lambda i: (i, 0))
    return pl.pallas_call(
        _final_kernel,
        grid=(n // TOKEN_TILE,),
        in_specs=[tile, _resident((1, D_MODEL))],
        out_specs=tile,
        out_shape=jax.ShapeDtypeStruct(x.shape, F32),
        compiler_params=_params("parallel"),
        name="final_norm",
    )(x, nw)


def _rope_tables(seq):
    t = jnp.arange(seq)
    row = (t // GRID_W).astype(F32)
    col = (t % GRID_W).astype(F32)
    half = HEAD_DIM // 2
    freqs = ROPE_THETA ** (-jnp.arange(0, half, 2, dtype=F32) / half)
    ang = jnp.concatenate([row[:, None] * freqs, col[:, None] * freqs], axis=-1)
    cos = jnp.repeat(jnp.cos(ang), 2, axis=-1)
    sin = jnp.repeat(jnp.sin(ang), 2, axis=-1)
    sign = jnp.tile(jnp.array([-1.0, 1.0], F32), half)
    return jnp.tile(cos, (1, 2)), jnp.tile(sin * sign, (1, 2))


def _chunk_cols(w):
    d, f = w.shape
    return w.reshape(d, f // FF_CHUNK, FF_CHUNK).transpose(1, 0, 2).astype(BF16)


def _chunk_rows(w):
    f, d = w.shape
    return w.reshape(f // FF_CHUNK, FF_CHUNK, d).astype(BF16)


def kernel(x, ffn_norm, mix_norm, ffn_w_gate, ffn_w_up, ffn_w_down, ab_w_in, ab_w_out,
           a_q_norm, a_k_norm, b_rpb, cd_w_in, cd_w_out, c_conv_w, d_conv_w,
           d_norm_g, d_norm_b, final_norm):
    bsz, seq, d = x.shape
    assert d == D_MODEL and seq % TOKEN_TILE == 0 and seq % GRID_W == 0
    n = bsz * seq
    rows = seq // GRID_W
    cos_t, sin_t = _rope_tables(seq)
    seg = jnp.asarray(np.kron(np.eye(LANES // HEAD_DIM), np.ones((HEAD_DIM, HEAD_DIM))), BF16)
    row_vec = lambda v: v.reshape(1, -1).astype(F32)

    def ffn(xf, i, k):
        return _ffn(xf, row_vec(ffn_norm[i, k]), _chunk_cols(ffn_w_gate[i, k]),
                    _chunk_cols(ffn_w_up[i, k]), _chunk_rows(ffn_w_down[i, k]))

    xf = x.reshape(n, d)
    for i in range(DEPTH):
        xf = ffn(xf, i, 0)
        j = i // 2
        if i % 2 == 0:
            two = lambda g: row_vec(jnp.tile(g, LANES // HEAD_DIM))
            aq, akd, avd, bq, bk, bv = _ab_in(
                xf, row_vec(mix_norm[i]), ab_w_in[j].astype(BF16), two(a_q_norm[j]),
                two(a_k_norm[j]), cos_t, sin_t, seg, seq)
            r3 = lambda a: a.reshape(bsz, seq, a.shape[1])
            ya = _gqa(r3(aq), r3(akd), r3(avd))
            yb = _na(r3(bq), r3(bk), r3(bv), _na_bias(b_rpb[j], rows))
            xf = _mix_out(xf, ya.reshape(n, A_Q), yb.reshape(n, B_W), ab_w_out[j].astype(BF16))
        else:
            zc, cb, zd = _cd_in(xf, row_vec(mix_norm[i]), cd_w_in[j].astype(BF16))
            r3 = lambda a: a.reshape(bsz, seq, a.shape[1])
            yc, yd = _conv(r3(zc), r3(cb), r3(zd), c_conv_w[j].astype(F32), d_conv_w[j].astype(F32),
                           row_vec(d_norm_g[j]), row_vec(d_norm_b[j]))
            xf = _mix_out(xf, yc.reshape(n, C_WIDTH), yd.reshape(n, D_WIDTH), cd_w_out[j].astype(BF16))
        xf = ffn(xf, i, 1)
    return _final_norm(xf, row_vec(final_norm)).reshape(bsz, seq, d)
```

```python
import functools

import jax
import jax.numpy as jnp
import numpy as np
from jax import lax
from jax.experimental import pallas as pl
from jax.experimental.pallas import tpu as pltpu

D_MODEL = 1024
DEPTH = 4
HEAD_DIM = 64
A_HEADS = 8
A_KV_HEADS = 2
B_HEADS = 8
C_WIDTH = 512
D_WIDTH = 512
C_CONV = 3
D_CONV = 31
D_FF = 2816
GRID_W = 64
NA_ROWS = 8
NA_COLS = 16
ROPE_THETA = 10000.0
EPS = 1e-6
NEG_INF = -1e30
A_Q = A_HEADS * HEAD_DIM
A_KV = A_KV_HEADS * HEAD_DIM
B_W = B_HEADS * HEAD_DIM

LANES = 128
VMEM_LIMIT_BYTES = 56 * 1024 * 1024
TOKEN_TILE = 512
FF_CHUNK = 256
Q_TILE = 512
CONV_ROWS = 128
D_HALO = 16

F32 = jnp.float32
BF16 = jnp.bfloat16
Q_SCALE = HEAD_DIM ** -0.5 * float(np.log2(np.e))


def _params(*sem):
    return pltpu.CompilerParams(dimension_semantics=sem, vmem_limit_bytes=VMEM_LIMIT_BYTES)


def _resident(shape):
    zeros = (0,) * len(shape)
    return pl.BlockSpec(shape, lambda *_: zeros, pipeline_mode=pl.Buffered(1))


def _rms_rows(x, gain):
    ms = jnp.mean(x * x, axis=-1, keepdims=True)
    return x * lax.rsqrt(ms + EPS) * gain


def _ffn_kernel(x_ref, nw_ref, wg_ref, wu_ref, wd_ref, fw_ref, o_ref, h_ref, acc_ref, *, final_norm):
    h_ref[...] = _rms_rows(x_ref[...], nw_ref[...]).astype(BF16)
    nch = D_FF // FF_CHUNK
    for c in range(nch):
        cols = slice(c * FF_CHUNK, (c + 1) * FF_CHUNK)
        h = h_ref[...]
        g = jnp.dot(h, wg_ref[:, cols], preferred_element_type=F32)
        u = jnp.dot(h, wu_ref[:, cols], preferred_element_type=F32)
        a = (g * jax.nn.sigmoid(g) * u).astype(BF16)
        d = jnp.dot(a, wd_ref[cols, :], preferred_element_type=F32)
        if c == 0:
            acc_ref[...] = d
        elif c < nch - 1:
            acc_ref[...] += d
        else:
            y = x_ref[...] + 0.5 * (acc_ref[...] + d)
            o_ref[...] = _rms_rows(y, fw_ref[...]) if final_norm else y


def _ffn(x, nw, wg, wu, wd, fw, final_norm):
    n = x.shape[0]
    tile = pl.BlockSpec((TOKEN_TILE, D_MODEL), lambda i: (i, 0))
    return pl.pallas_call(
        functools.partial(_ffn_kernel, final_norm=final_norm),
        grid=(n // TOKEN_TILE,),
        in_specs=[tile, _resident((1, D_MODEL)), _resident(wg.shape), _resident(wu.shape),
                  _resident(wd.shape), _resident((1, D_MODEL))],
        out_specs=tile,
        out_shape=jax.ShapeDtypeStruct(x.shape, F32),
        scratch_shapes=[pltpu.VMEM((TOKEN_TILE, D_MODEL), BF16),
                        pltpu.VMEM((TOKEN_TILE, D_MODEL), F32)],
        compiler_params=_params("parallel"),
        name="ffn",
    )(x, nw, wg, wu, wd, fw)


def _swap_pairs(x):
    lane = lax.broadcasted_iota(jnp.int32, x.shape, 1)
    return jnp.where(lane % 2 == 0, pltpu.roll(x, LANES - 1, 1), pltpu.roll(x, 1, 1))


def _head_norm_rope(xg, gain, cos, sin_signed, seg_ones):
    sq = xg * xg
    hi = sq.astype(BF16)
    lo = (sq - hi.astype(F32)).astype(BF16)
    ssum = jnp.dot(jnp.concatenate([hi, lo], axis=1), seg_ones, preferred_element_type=F32)
    xn = xg * lax.rsqrt(ssum * (1.0 / HEAD_DIM) + EPS) * gain
    return xn * cos + _swap_pairs(xn) * sin_signed


def _dup_halves(x):
    lane = lax.broadcasted_iota(jnp.int32, x.shape, 1)
    rolled = pltpu.roll(x, HEAD_DIM, 1)
    low = lane < HEAD_DIM
    return jnp.where(low, x, rolled), jnp.where(low, rolled, x)


def _ab_in_kernel(x_ref, nw_ref, w_ref, qg_ref, kg_ref, cos_ref, sin_ref, seg_ref,
                  aq_ref, ak_ref, av_ref, bq_ref, bk_ref, bv_ref, h_ref):
    h_ref[...] = _rms_rows(x_ref[...], nw_ref[...]).astype(BF16)
    h = h_ref[...]
    cos = cos_ref[...]
    sin = sin_ref[...]
    seg = seg_ref[...]

    def proj(lo, width):
        return jnp.dot(h, w_ref[:, lo:lo + width], preferred_element_type=F32)

    aq = proj(0, A_Q)
    for j in range(A_Q // LANES):
        q = _head_norm_rope(aq[:, j * LANES:(j + 1) * LANES], qg_ref[...], cos, sin, seg)
        aq_ref[:, j * LANES:(j + 1) * LANES] = (q * Q_SCALE).astype(BF16)
    akv = proj(A_Q, 2 * A_KV)
    k0, k1 = _dup_halves(_head_norm_rope(akv[:, :A_KV], kg_ref[...], cos, sin, seg))
    ak_ref[:, :LANES] = k0.astype(BF16)
    ak_ref[:, LANES:] = k1.astype(BF16)
    v0, v1 = _dup_halves(akv[:, A_KV:])
    av_ref[:, :LANES] = v0.astype(BF16)
    av_ref[:, LANES:] = v1.astype(BF16)
    base = A_Q + 2 * A_KV
    bq_ref[...] = (proj(base, B_W) * Q_SCALE).astype(BF16)
    bk_ref[...] = proj(base + B_W, B_W).astype(BF16)
    bv_ref[...] = proj(base + 2 * B_W, B_W).astype(BF16)


def _ab_in(x, nw, w_in, qg, kg, cos_t, sin_t, seg, seq):
    n = x.shape[0]
    ab_in = w_in.shape[1]
    tiles_per_seq = seq // TOKEN_TILE
    tile = lambda w: pl.BlockSpec((TOKEN_TILE, w), lambda i: (i, 0))
    rope = pl.BlockSpec((TOKEN_TILE, LANES), lambda i: (i % tiles_per_seq, 0))
    widths = (A_Q, 2 * LANES, 2 * LANES, B_W, B_W, B_W)
    return pl.pallas_call(
        _ab_in_kernel,
        grid=(n // TOKEN_TILE,),
        in_specs=[tile(D_MODEL), _resident((1, D_MODEL)), _resident((D_MODEL, ab_in)),
                  _resident((1, LANES)), _resident((1, LANES)), rope, rope,
                  _resident((2 * LANES, LANES))],
        out_specs=[tile(w) for w in widths],
        out_shape=[jax.ShapeDtypeStruct((n, w), BF16) for w in widths],
        scratch_shapes=[pltpu.VMEM((TOKEN_TILE, D_MODEL), BF16)],
        compiler_params=_params("parallel"),
        name="ab_in",
    )(x, nw, w_in, qg, kg, cos_t, sin_t, seg)


def _stack_heads(qp):
    lane = lax.broadcasted_iota(jnp.int32, qp.shape, 1)
    zero = jnp.zeros_like(qp)
    return jnp.concatenate([jnp.where(lane < HEAD_DIM, qp, zero),
                            jnp.where(lane < HEAD_DIM, zero, qp)], axis=0)


def _unstack_heads(o, rows):
    lane = lax.broadcasted_iota(jnp.int32, (rows, LANES), 1)
    return jnp.where(lane < HEAD_DIM, o[:rows], o[rows:])


def _softmax_pv(s, v):
    m = jnp.max(s, axis=-1, keepdims=True)
    p = jnp.exp2(s - m)
    l = jnp.sum(p, axis=-1, keepdims=True)
    return jnp.dot(p.astype(BF16), v, preferred_element_type=F32) / l


_NT = (((1,), (1,)), ((), ()))


def _gqa_kernel(q_ref, k_ref, v_ref, o_ref):
    k = k_ref[0]
    v = v_ref[0]
    lane = lax.broadcasted_iota(jnp.int32, (Q_TILE, LANES), 1)
    for j in range(2):
        qp = q_ref[0, :, j * LANES:(j + 1) * LANES]
        zero = jnp.zeros_like(qp)
        outs = []
        for keep in (lane < HEAD_DIM, lane >= HEAD_DIM):
            s = lax.dot_general(jnp.where(keep, qp, zero), k, _NT, preferred_element_type=F32)
            outs.append(_softmax_pv(s, v))
        o_ref[0, :, j * LANES:(j + 1) * LANES] = jnp.where(lane < HEAD_DIM, outs[0], outs[1]).astype(BF16)


def _gqa(aq, akd, avd):
    b, s, _ = aq.shape
    qspec = pl.BlockSpec((1, Q_TILE, 2 * LANES), lambda bi, g, qi: (bi, qi, g))
    kvspec = pl.BlockSpec((1, s, LANES), lambda bi, g, qi: (bi, 0, g))
    return pl.pallas_call(
        _gqa_kernel,
        grid=(b, A_KV_HEADS, s // Q_TILE),
        in_specs=[qspec, kvspec, kvspec],
        out_specs=qspec,
        out_shape=jax.ShapeDtypeStruct(aq.shape, BF16),
        compiler_params=_params("parallel", "parallel", "arbitrary"),
        name="gqa",
    )(aq, akd, avd)


def _na_kernel(q_ref, k_ref, v_ref, bias_ref, o_ref, *, rows):
    band = NA_ROWS * GRID_W

    def row(r, carry):
        rs = jnp.clip(r - NA_ROWS // 2, 0, rows - NA_ROWS)
        d = r - rs
        q0 = pl.multiple_of(r * GRID_W, GRID_W)
        k0 = pl.multiple_of(rs * GRID_W, GRID_W)
        for j in range(B_W // LANES):
            cols = slice(j * LANES, (j + 1) * LANES)
            qs = _stack_heads(q_ref[0, pl.ds(q0, GRID_W), cols])
            kb = k_ref[0, pl.ds(k0, band), cols]
            vb = v_ref[0, pl.ds(k0, band), cols]
            s = lax.dot_general(qs, kb, _NT, preferred_element_type=F32) + bias_ref[d, j]
            o = _softmax_pv(s, vb)
            o_ref[0, pl.ds(q0, GRID_W), cols] = _unstack_heads(o, GRID_W).astype(BF16)
        return carry

    lax.fori_loop(0, rows, row, 0, unroll=2)


def _na(bq, bk, bv, bias):
    b, s, w = bq.shape
    rows = s // GRID_W
    spec = pl.BlockSpec((1, s, w), lambda bi: (bi, 0, 0))
    return pl.pallas_call(
        functools.partial(_na_kernel, rows=rows),
        grid=(b,),
        in_specs=[spec, spec, spec, _resident(bias.shape)],
        out_specs=spec,
        out_shape=jax.ShapeDtypeStruct(bq.shape, BF16),
        compiler_params=_params("parallel"),
        name="na",
    )(bq, bk, bv, bias)


def _na_bias(rpb, rows):
    wr = min(NA_ROWS, rows)
    qc = np.arange(GRID_W)[:, None]
    kc = np.arange(GRID_W)[None, :]
    col_off = np.clip(kc - qc, -(NA_COLS - 1), NA_COLS - 1) + NA_COLS - 1
    wcs = np.clip(qc - NA_COLS // 2, 0, GRID_W - NA_COLS)
    valid = (kc >= wcs) & (kc < wcs + NA_COLS)
    onehot = (col_off[None] == np.arange(2 * NA_COLS - 1)[:, None, None]).astype(np.float32)
    by_col = jnp.einsum("hrc,cqk->hrqk", rpb.astype(F32), jnp.asarray(onehot),
                        precision=lax.Precision.HIGHEST)
    t = jnp.stack([by_col[:, NA_ROWS - 1 - d:NA_ROWS - 1 - d + wr] for d in range(wr)])
    t = jnp.where(valid[None, None, None], t * float(np.log2(np.e)), NEG_INF)
    t = t.transpose(0, 1, 3, 2, 4)
    return t.reshape(wr, B_HEADS // 2, 2 * GRID_W, wr * GRID_W)


def _out_kernel(x_ref, ya_ref, yb_ref, w_ref, o_ref):
    half = ya_ref.shape[1]
    o_ref[...] = (x_ref[...]
                  + jnp.dot(ya_ref[...], w_ref[:half], preferred_element_type=F32)
                  + jnp.dot(yb_ref[...], w_ref[half:], preferred_element_type=F32))


def _mix_out(x, ya, yb, w_out):
    n = x.shape[0]
    tile = lambda w: pl.BlockSpec((TOKEN_TILE, w), lambda i: (i, 0))
    return pl.pallas_call(
        _out_kernel,
        grid=(n // TOKEN_TILE,),
        in_specs=[tile(D_MODEL), tile(ya.shape[1]), tile(yb.shape[1]), _resident(w_out.shape)],
        out_specs=tile(D_MODEL),
        out_shape=jax.ShapeDtypeStruct(x.shape, F32),
        compiler_params=_params("parallel"),
        name="mix_out",
    )(x, ya, yb, w_out)


def _cd_in_kernel(x_ref, nw_ref, w_ref, zc_ref, cb_ref, zd_ref, h_ref):
    h_ref[...] = _rms_rows(x_ref[...], nw_ref[...]).astype(BF16)
    h = h_ref[...]

    def proj(i):
        return jnp.dot(h, w_ref[:, i * C_WIDTH:(i + 1) * C_WIDTH], preferred_element_type=F32)

    zc_ref[...] = (proj(2) * proj(0)).astype(BF16)
    cb_ref[...] = proj(1).astype(BF16)
    zd_ref[...] = (proj(3) * jax.nn.sigmoid(proj(4))).astype(BF16)


def _cd_in(x, nw, w_in):
    n = x.shape[0]
    tile = lambda w: pl.BlockSpec((TOKEN_TILE, w), lambda i: (i, 0))
    return pl.pallas_call(
        _cd_in_kernel,
        grid=(n // TOKEN_TILE,),
        in_specs=[tile(D_MODEL), _resident((1, D_MODEL)), _resident(w_in.shape)],
        out_specs=[tile(C_WIDTH)] * 3,
        out_shape=[jax.ShapeDtypeStruct((n, C_WIDTH), BF16)] * 3,
        scratch_shapes=[pltpu.VMEM((TOKEN_TILE, D_MODEL), BF16)],
        compiler_params=_params("parallel"),
        name="cd_in",
    )(x, nw, w_in)


def _conv_kernel(zc_ref, cb_ref, zd_ref, wc_ref, wd_ref, g_ref, b_ref, yc_ref, yd_ref,
                 pad_ref, raw_ref, *, seq):
    halo = jnp.zeros((D_HALO, C_WIDTH), F32)
    nchunks = seq // CONV_ROWS
    window = CONV_ROWS + 2 * D_HALO

    def conv(src_ref, w_ref, taps, emit):
        pad_ref[:D_HALO] = halo
        pad_ref[D_HALO + seq:] = halo
        pad_ref[D_HALO:D_HALO + seq] = src_ref[0].astype(F32)
        first = D_HALO - taps // 2
        for c in range(C_WIDTH // LANES):
            cols = slice(c * LANES, (c + 1) * LANES)
            w = w_ref[:, cols]

            def chunk(i, carry):
                r0 = pl.multiple_of(i * CONV_ROWS, CONV_ROWS)
                win = pad_ref[pl.ds(r0, window), cols]
                acc = jnp.zeros((CONV_ROWS, LANES), F32)
                for sh in range(8):
                    offs = [first + k for k in range(taps) if (first + k) % 8 == sh]
                    if not offs:
                        continue
                    shifted = win if sh == 0 else pltpu.roll(win, window - sh, 0)
                    for off in offs:
                        k = off - first
                        base = off - sh
                        acc = acc + w[k:k + 1, :] * shifted[base:base + CONV_ROWS]
                emit(r0, cols, acc)
                return carry

            lax.fori_loop(0, nchunks, chunk, 0)

    def emit_c(r0, cols, acc):
        yc_ref[0, pl.ds(r0, CONV_ROWS), cols] = (
            cb_ref[0, pl.ds(r0, CONV_ROWS), cols].astype(F32) * acc).astype(BF16)

    def emit_d(r0, cols, acc):
        raw_ref[pl.ds(r0, CONV_ROWS), cols] = acc

    conv(zc_ref, wc_ref, C_CONV, emit_c)
    conv(zd_ref, wd_ref, D_CONV, emit_d)

    def norm(i, carry):
        r0 = pl.multiple_of(i * CONV_ROWS, CONV_ROWS)
        y = raw_ref[pl.ds(r0, CONV_ROWS), :]
        mu = jnp.mean(y, axis=-1, keepdims=True)
        var = jnp.mean(jnp.square(y - mu), axis=-1, keepdims=True)
        z = (y - mu) * lax.rsqrt(var + EPS) * g_ref[...] + b_ref[...]
        yd_ref[0, pl.ds(r0, CONV_ROWS), :] = (z * jax.nn.sigmoid(z)).astype(BF16)
        return carry

    lax.fori_loop(0, nchunks, norm, 0, unroll=4)


def _conv(zc, cb, zd, wc, wd, g, b):
    bsz, seq, w = zc.shape
    padded = seq + 2 * D_HALO
    spec = pl.BlockSpec((1, seq, w), lambda bi: (bi, 0, 0))
    return pl.pallas_call(
        functools.partial(_conv_kernel, seq=seq),
        grid=(bsz,),
        in_specs=[spec, spec, spec, _resident(wc.shape), _resident(wd.shape),
                  _resident((1, w)), _resident((1, w))],
        out_specs=[spec, spec],
        out_shape=[jax.ShapeDtypeStruct(zc.shape, BF16)] * 2,
        scratch_shapes=[pltpu.VMEM((padded, w), F32),
                        pltpu.VMEM((seq, w), F32)],
        compiler_params=_params("parallel"),
        name="conv",
    )(zc, cb, zd, wc, wd, g, b)


def _rope_tables(seq):
    t = jnp.arange(seq)
    row = (t // GRID_W).astype(F32)
    col = (t % GRID_W).astype(F32)
    half = HEAD_DIM // 2
    freqs = ROPE_THETA ** (-jnp.arange(0, half, 2, dtype=F32) / half)
    ang = jnp.concatenate([row[:, None] * freqs, col[:, None] * freqs], axis=-1)
    cos = jnp.repeat(jnp.cos(ang), 2, axis=-1)
    sin = jnp.repeat(jnp.sin(ang), 2, axis=-1)
    sign = jnp.tile(jnp.array([-1.0, 1.0], F32), half)
    return jnp.tile(cos, (1, 2)), jnp.tile(sin * sign, (1, 2))


def kernel(x, ffn_norm, mix_norm, ffn_w_gate, ffn_w_up, ffn_w_down, ab_w_in, ab_w_out,
           a_q_norm, a_k_norm, b_rpb, cd_w_in, cd_w_out, c_conv_w, d_conv_w,
           d_norm_g, d_norm_b, final_norm):
    bsz, seq, d = x.shape
    assert d == D_MODEL and seq % TOKEN_TILE == 0 and seq % GRID_W == 0
    n = bsz * seq
    rows = seq // GRID_W
    cos_t, sin_t = _rope_tables(seq)
    seg = jnp.asarray(np.tile(np.kron(np.eye(LANES // HEAD_DIM), np.ones((HEAD_DIM, HEAD_DIM))), (2, 1)), BF16)
    row_vec = lambda v: v.reshape(1, -1).astype(F32)

    def ffn(xf, i, k):
        last = i == DEPTH - 1 and k == 1
        return _ffn(xf, row_vec(ffn_norm[i, k]), ffn_w_gate[i, k].astype(BF16),
                    ffn_w_up[i, k].astype(BF16), ffn_w_down[i, k].astype(BF16),
                    row_vec(final_norm), final_norm=last)

    xf = x.reshape(n, d)
    for i in range(DEPTH):
        xf = ffn(xf, i, 0)
        j = i // 2
        if i % 2 == 0:
            two = lambda g: row_vec(jnp.tile(g, LANES // HEAD_DIM))
            aq, akd, avd, bq, bk, bv = _ab_in(
                xf, row_vec(mix_norm[i]), ab_w_in[j].astype(BF16), two(a_q_norm[j]),
                two(a_k_norm[j]), cos_t, sin_t, seg, seq)
            r3 = lambda a: a.reshape(bsz, seq, a.shape[1])
            ya = _gqa(r3(aq), r3(akd), r3(avd))
            yb = _na(r3(bq), r3(bk), r3(bv), _na_bias(b_rpb[j], rows))
            xf = _mix_out(xf, ya.reshape(n, A_Q), yb.reshape(n, B_W), ab_w_out[j].astype(BF16))
        else:
            zc, cb, zd = _cd_in(xf, row_vec(mix_norm[i]), cd_w_in[j].astype(BF16))
            r3 = lambda a: a.reshape(bsz, seq, a.shape[1])
            yc, yd = _conv(r3(zc), r3(cb), r3(zd), c_conv_w[j].astype(F32), d_conv_w[j].astype(F32),
                           row_vec(d_norm_g[j]), row_vec(d_norm_b[j]))
            xf = _mix_out(xf, yc.reshape(n, C_WIDTH), yd.reshape(n, D_WIDTH), cd_w_out[j].astype(BF16))
        xf = ffn(xf, i, 1)
    return xf.reshape(bsz, seq, d)
```

```python
import functools

import jax
import jax.numpy as jnp
import numpy as np
from jax import lax
from jax.experimental import pallas as pl
from jax.experimental.pallas import tpu as pltpu

D_MODEL = 1024
DEPTH = 4
HEAD_DIM = 64
A_HEADS = 8
A_KV_HEADS = 2
B_HEADS = 8
C_WIDTH = 512
D_WIDTH = 512
C_CONV = 3
D_CONV = 31
D_FF = 2816
GRID_W = 64
NA_ROWS = 8
NA_COLS = 16
ROPE_THETA = 10000.0
EPS = 1e-6
NEG_INF = -1e30
A_Q = A_HEADS * HEAD_DIM
A_KV = A_KV_HEADS * HEAD_DIM
B_W = B_HEADS * HEAD_DIM

LANES = 128
VMEM_LIMIT_BYTES = 56 * 1024 * 1024
TOKEN_TILE = 512
FF_CHUNK = 256
Q_TILE = 1024
Q_CHAIN = 512
NA_ROWS_PER_STEP = 4
CONV_ROWS = 128
D_HALO = 16

F32 = jnp.float32
BF16 = jnp.bfloat16
Q_SCALE = HEAD_DIM ** -0.5 * float(np.log2(np.e))


def _params(*sem):
    return pltpu.CompilerParams(dimension_semantics=sem, vmem_limit_bytes=VMEM_LIMIT_BYTES)


def _resident(shape):
    zeros = (0,) * len(shape)
    return pl.BlockSpec(shape, lambda *_: zeros, pipeline_mode=pl.Buffered(1))


def _rms_rows(x, gain):
    ms = jnp.mean(x * x, axis=-1, keepdims=True)
    return x * lax.rsqrt(ms + EPS) * gain


def _ffn_kernel(*refs, final_norm, mixed):
    if mixed:
        x_ref, ya_ref, yb_ref, wo_ref, nw_ref, wg_ref, wu_ref, wd_ref, fw_ref, o_ref, h_ref, acc_ref, x1_ref = refs
        half = ya_ref.shape[1]
        x1_ref[...] = (x_ref[...]
                       + jnp.dot(ya_ref[...], wo_ref[:half], preferred_element_type=F32)
                       + jnp.dot(yb_ref[...], wo_ref[half:], preferred_element_type=F32))
    else:
        x_ref, nw_ref, wg_ref, wu_ref, wd_ref, fw_ref, o_ref, h_ref, acc_ref = refs
        x1_ref = x_ref
    h_ref[...] = _rms_rows(x1_ref[...], nw_ref[...]).astype(BF16)
    nch = D_FF // FF_CHUNK
    for c in range(nch):
        cols = slice(c * FF_CHUNK, (c + 1) * FF_CHUNK)
        h = h_ref[...]
        g = jnp.dot(h, wg_ref[:, cols], preferred_element_type=F32)
        u = jnp.dot(h, wu_ref[:, cols], preferred_element_type=F32)
        a = (g * jax.nn.sigmoid(g) * u).astype(BF16)
        d = jnp.dot(a, wd_ref[cols, :], preferred_element_type=F32)
        if c == 0:
            acc_ref[...] = d
        elif c < nch - 1:
            acc_ref[...] += d
        else:
            y = x1_ref[...] + 0.5 * (acc_ref[...] + d)
            o_ref[...] = _rms_rows(y, fw_ref[...]) if final_norm else y


def _ffn(x, nw, wg, wu, wd, fw, final_norm, mix=None):
    n = x.shape[0]
    tile = lambda w: pl.BlockSpec((TOKEN_TILE, w), lambda i: (i, 0))
    ffn_specs = [_resident((1, D_MODEL)), _resident(wg.shape), _resident(wu.shape),
                 _resident(wd.shape), _resident((1, D_MODEL))]
    scratch = [pltpu.VMEM((TOKEN_TILE, D_MODEL), BF16), pltpu.VMEM((TOKEN_TILE, D_MODEL), F32)]
    if mix is None:
        operands, mix_specs = (x,), [tile(D_MODEL)]
    else:
        ya, yb, wo = mix
        operands = (x, ya, yb, wo)
        mix_specs = [tile(D_MODEL), tile(ya.shape[1]), tile(yb.shape[1]), _resident(wo.shape)]
        scratch.append(pltpu.VMEM((TOKEN_TILE, D_MODEL), F32))
    return pl.pallas_call(
        functools.partial(_ffn_kernel, final_norm=final_norm, mixed=mix is not None),
        grid=(n // TOKEN_TILE,),
        in_specs=mix_specs + ffn_specs,
        out_specs=tile(D_MODEL),
        out_shape=jax.ShapeDtypeStruct(x.shape, F32),
        scratch_shapes=scratch,
        compiler_params=_params("parallel"),
        name="ffn_mix" if mix is not None else "ffn",
    )(*operands, nw, wg, wu, wd, fw)


def _swap_pairs(x):
    lane = lax.broadcasted_iota(jnp.int32, x.shape, 1)
    return jnp.where(lane % 2 == 0, pltpu.roll(x, LANES - 1, 1), pltpu.roll(x, 1, 1))


def _head_norm_rope(xg, gain, cos, sin_signed, seg_ones):
    sq = xg * xg
    hi = sq.astype(BF16)
    lo = (sq - hi.astype(F32)).astype(BF16)
    ssum = jnp.dot(jnp.concatenate([hi, lo], axis=1), seg_ones, preferred_element_type=F32)
    xn = xg * lax.rsqrt(ssum * (1.0 / HEAD_DIM) + EPS) * gain
    return xn * cos + _swap_pairs(xn) * sin_signed


def _dup_halves(x):
    lane = lax.broadcasted_iota(jnp.int32, x.shape, 1)
    rolled = pltpu.roll(x, HEAD_DIM, 1)
    low = lane < HEAD_DIM
    return jnp.where(low, x, rolled), jnp.where(low, rolled, x)


def _ab_in_kernel(x_ref, nw_ref, w_ref, qg_ref, kg_ref, cos_ref, sin_ref, seg_ref,
                  aq_ref, ak_ref, av_ref, bq_ref, bk_ref, bv_ref, h_ref):
    h_ref[...] = _rms_rows(x_ref[...], nw_ref[...]).astype(BF16)
    h = h_ref[...]
    cos = cos_ref[...]
    sin = sin_ref[...]
    seg = seg_ref[...]

    def proj(lo, width):
        return jnp.dot(h, w_ref[:, lo:lo + width], preferred_element_type=F32)

    aq = proj(0, A_Q)
    for j in range(A_Q // LANES):
        q = _head_norm_rope(aq[:, j * LANES:(j + 1) * LANES], qg_ref[...], cos, sin, seg)
        aq_ref[:, j * LANES:(j + 1) * LANES] = (q * Q_SCALE).astype(BF16)
    akv = proj(A_Q, 2 * A_KV)
    k0, k1 = _dup_halves(_head_norm_rope(akv[:, :A_KV], kg_ref[...], cos, sin, seg))
    ak_ref[:, :LANES] = k0.astype(BF16)
    ak_ref[:, LANES:] = k1.astype(BF16)
    v0, v1 = _dup_halves(akv[:, A_KV:])
    av_ref[:, :LANES] = v0.astype(BF16)
    av_ref[:, LANES:] = v1.astype(BF16)
    base = A_Q + 2 * A_KV
    bq_ref[...] = (proj(base, B_W) * Q_SCALE).astype(BF16)
    bk_ref[...] = proj(base + B_W, B_W).astype(BF16)
    bv_ref[...] = proj(base + 2 * B_W, B_W).astype(BF16)


def _ab_in(x, nw, w_in, qg, kg, cos_t, sin_t, seg, seq):
    n = x.shape[0]
    ab_in = w_in.shape[1]
    tiles_per_seq = seq // TOKEN_TILE
    tile = lambda w: pl.BlockSpec((TOKEN_TILE, w), lambda i: (i, 0))
    rope = pl.BlockSpec((TOKEN_TILE, LANES), lambda i: (i % tiles_per_seq, 0))
    widths = (A_Q, 2 * LANES, 2 * LANES, B_W, B_W, B_W)
    return pl.pallas_call(
        _ab_in_kernel,
        grid=(n // TOKEN_TILE,),
        in_specs=[tile(D_MODEL), _resident((1, D_MODEL)), _resident((D_MODEL, ab_in)),
                  _resident((1, LANES)), _resident((1, LANES)), rope, rope,
                  _resident((2 * LANES, LANES))],
        out_specs=[tile(w) for w in widths],
        out_shape=[jax.ShapeDtypeStruct((n, w), BF16) for w in widths],
        scratch_shapes=[pltpu.VMEM((TOKEN_TILE, D_MODEL), BF16)],
        compiler_params=_params("parallel"),
        name="ab_in",
    )(x, nw, w_in, qg, kg, cos_t, sin_t, seg)


def _stack_heads(qp):
    lane = lax.broadcasted_iota(jnp.int32, qp.shape, 1)
    zero = jnp.zeros_like(qp)
    return jnp.concatenate([jnp.where(lane < HEAD_DIM, qp, zero),
                            jnp.where(lane < HEAD_DIM, zero, qp)], axis=0)


def _unstack_heads(o, rows):
    lane = lax.broadcasted_iota(jnp.int32, (rows, LANES), 1)
    return jnp.where(lane < HEAD_DIM, o[:rows], o[rows:])


def _softmax_pv(s, v):
    m = jnp.max(s, axis=-1, keepdims=True)
    p = jnp.exp2(s - m)
    l = jnp.sum(p, axis=-1, keepdims=True)
    return jnp.dot(p.astype(BF16), v, preferred_element_type=F32) / l


_NT = (((1,), (1,)), ((), ()))


def _gqa_kernel(q_ref, k_ref, v_ref, o_ref):
    k = k_ref[0]
    v = v_ref[0]
    lane = lax.broadcasted_iota(jnp.int32, (Q_CHAIN, LANES), 1)
    masks = (lane < HEAD_DIM, lane >= HEAD_DIM)
    chains = [(blk, h) for blk in range(Q_TILE // Q_CHAIN) for h in range(A_HEADS // A_KV_HEADS)]

    def block(c):
        blk, h = chains[c]
        return slice(blk * Q_CHAIN, (blk + 1) * Q_CHAIN), slice((h // 2) * LANES, (h // 2 + 1) * LANES)

    def scores(c):
        rows, cols = block(c)
        qp = q_ref[0, rows, cols]
        qh = jnp.where(masks[chains[c][1] % 2], qp, jnp.zeros_like(qp))
        return lax.dot_general(qh, k, _NT, preferred_element_type=F32)

    s_next = scores(0)
    outs = []
    for c in range(len(chains)):
        s_cur = s_next
        if c + 1 < len(chains):
            s_next = scores(c + 1)
        outs.append(_softmax_pv(s_cur, v))
        if c % 2 == 1:
            rows, cols = block(c)
            o_ref[0, rows, cols] = jnp.where(lane < HEAD_DIM, outs[c - 1], outs[c]).astype(BF16)


def _gqa(aq, akd, avd):
    b, s, _ = aq.shape
    qspec = pl.BlockSpec((1, Q_TILE, 2 * LANES), lambda bi, g, qi: (bi, qi, g))
    kvspec = pl.BlockSpec((1, s, LANES), lambda bi, g, qi: (bi, 0, g))
    return pl.pallas_call(
        _gqa_kernel,
        grid=(b, A_KV_HEADS, s // Q_TILE),
        in_specs=[qspec, kvspec, kvspec],
        out_specs=qspec,
        out_shape=jax.ShapeDtypeStruct(aq.shape, BF16),
        compiler_params=_params("parallel", "parallel", "arbitrary"),
        name="gqa",
    )(aq, akd, avd)


def _na_kernel(q_ref, k_ref, v_ref, bias_ref, o_ref, *, rows):
    band = NA_ROWS * GRID_W

    pairs = B_W // LANES

    def row_group(i, carry):
        chains = []
        for rr in range(NA_ROWS_PER_STEP):
            r = i * NA_ROWS_PER_STEP + rr
            rs = jnp.clip(r - NA_ROWS // 2, 0, rows - NA_ROWS)
            q0 = pl.multiple_of(r * GRID_W, GRID_W)
            k0 = pl.multiple_of(rs * GRID_W, GRID_W)
            chains += [(r - rs, q0, k0, j) for j in range(pairs)]

        def scores(c):
            d, q0, k0, j = chains[c]
            cols = slice(j * LANES, (j + 1) * LANES)
            qs = _stack_heads(q_ref[0, pl.ds(q0, GRID_W), cols])
            kb = k_ref[0, pl.ds(k0, band), cols]
            return lax.dot_general(qs, kb, _NT, preferred_element_type=F32) + bias_ref[d, j]

        s_next = scores(0)
        for c in range(len(chains)):
            s_cur = s_next
            if c + 1 < len(chains):
                s_next = scores(c + 1)
            d, q0, k0, j = chains[c]
            cols = slice(j * LANES, (j + 1) * LANES)
            o = _softmax_pv(s_cur, v_ref[0, pl.ds(k0, band), cols])
            o_ref[0, pl.ds(q0, GRID_W), cols] = _unstack_heads(o, GRID_W).astype(BF16)
        return carry

    lax.fori_loop(0, rows // NA_ROWS_PER_STEP, row_group, 0)


def _na(bq, bk, bv, bias):
    b, s, w = bq.shape
    rows = s // GRID_W
    spec = pl.BlockSpec((1, s, w), lambda bi: (bi, 0, 0))
    return pl.pallas_call(
        functools.partial(_na_kernel, rows=rows),
        grid=(b,),
        in_specs=[spec, spec, spec, _resident(bias.shape)],
        out_specs=spec,
        out_shape=jax.ShapeDtypeStruct(bq.shape, BF16),
        compiler_params=_params("parallel"),
        name="na",
    )(bq, bk, bv, bias)


def _na_bias(rpb, rows):
    wr = min(NA_ROWS, rows)
    qc = np.arange(GRID_W)[:, None]
    kc = np.arange(GRID_W)[None, :]
    col_off = np.clip(kc - qc, -(NA_COLS - 1), NA_COLS - 1) + NA_COLS - 1
    wcs = np.clip(qc - NA_COLS // 2, 0, GRID_W - NA_COLS)
    valid = (kc >= wcs) & (kc < wcs + NA_COLS)
    onehot = (col_off[None] == np.arange(2 * NA_COLS - 1)[:, None, None]).astype(np.float32)
    by_col = jnp.einsum("hrc,cqk->hrqk", rpb.astype(F32), jnp.asarray(onehot),
                        precision=lax.Precision.HIGHEST)
    t = jnp.stack([by_col[:, NA_ROWS - 1 - d:NA_ROWS - 1 - d + wr] for d in range(wr)])
    t = jnp.where(valid[None, None, None], t * float(np.log2(np.e)), NEG_INF)
    t = t.transpose(0, 1, 3, 2, 4)
    return t.reshape(wr, B_HEADS // 2, 2 * GRID_W, wr * GRID_W)


def _cd_in_kernel(x_ref, nw_ref, w_ref, zc_ref, cb_ref, zd_ref, h_ref):
    h_ref[...] = _rms_rows(x_ref[...], nw_ref[...]).astype(BF16)
    h = h_ref[...]

    def proj(i):
        return jnp.dot(h, w_ref[:, i * C_WIDTH:(i + 1) * C_WIDTH], preferred_element_type=F32)

    zc_ref[...] = (proj(2) * proj(0)).astype(BF16)
    cb_ref[...] = proj(1).astype(BF16)
    zd_ref[...] = (proj(3) * jax.nn.sigmoid(proj(4))).astype(BF16)


def _cd_in(x, nw, w_in):
    n = x.shape[0]
    tile = lambda w: pl.BlockSpec((TOKEN_TILE, w), lambda i: (i, 0))
    return pl.pallas_call(
        _cd_in_kernel,
        grid=(n // TOKEN_TILE,),
        in_specs=[tile(D_MODEL), _resident((1, D_MODEL)), _resident(w_in.shape)],
        out_specs=[tile(C_WIDTH)] * 3,
        out_shape=[jax.ShapeDtypeStruct((n, C_WIDTH), BF16)] * 3,
        scratch_shapes=[pltpu.VMEM((TOKEN_TILE, D_MODEL), BF16)],
        compiler_params=_params("parallel"),
        name="cd_in",
    )(x, nw, w_in)


def _conv_kernel(zc_ref, cb_ref, zd_ref, wc_ref, wd_ref, g_ref, b_ref, yc_ref, yd_ref,
                 pad_ref, raw_ref, *, seq):
    halo = jnp.zeros((D_HALO, C_WIDTH), F32)
    nchunks = seq // CONV_ROWS
    window = CONV_ROWS + 2 * D_HALO

    def conv(src_ref, w_ref, taps, emit):
        pad_ref[:D_HALO] = halo
        pad_ref[D_HALO + seq:] = halo
        pad_ref[D_HALO:D_HALO + seq] = src_ref[0].astype(F32)
        first = D_HALO - taps // 2
        for c in range(C_WIDTH // LANES):
            cols = slice(c * LANES, (c + 1) * LANES)
            w = w_ref[:, cols]

            def chunk(i, carry):
                r0 = pl.multiple_of(i * CONV_ROWS, CONV_ROWS)
                win = pad_ref[pl.ds(r0, window), cols]
                acc = jnp.zeros((CONV_ROWS, LANES), F32)
                for sh in range(8):
                    offs = [first + k for k in range(taps) if (first + k) % 8 == sh]
                    if not offs:
                        continue
                    shifted = win if sh == 0 else pltpu.roll(win, window - sh, 0)
                    for off in offs:
                        k = off - first
                        base = off - sh
                        acc = acc + w[k:k + 1, :] * shifted[base:base + CONV_ROWS]
                emit(r0, cols, acc)
                return carry

            lax.fori_loop(0, nchunks, chunk, 0)

    def emit_c(r0, cols, acc):
        yc_ref[0, pl.ds(r0, CONV_ROWS), cols] = (
            cb_ref[0, pl.ds(r0, CONV_ROWS), cols].astype(F32) * acc).astype(BF16)

    def emit_d(r0, cols, acc):
        raw_ref[pl.ds(r0, CONV_ROWS), cols] = acc

    conv(zc_ref, wc_ref, C_CONV, emit_c)
    conv(zd_ref, wd_ref, D_CONV, emit_d)

    def norm(i, carry):
        r0 = pl.multiple_of(i * CONV_ROWS, CONV_ROWS)
        y = raw_ref[pl.ds(r0, CONV_ROWS), :]
        mu = jnp.mean(y, axis=-1, keepdims=True)
        var = jnp.mean(jnp.square(y - mu), axis=-1, keepdims=True)
        z = (y - mu) * lax.rsqrt(var + EPS) * g_ref[...] + b_ref[...]
        yd_ref[0, pl.ds(r0, CONV_ROWS), :] = (z * jax.nn.sigmoid(z)).astype(BF16)
        return carry

    lax.fori_loop(0, nchunks, norm, 0, unroll=4)


def _conv(zc, cb, zd, wc, wd, g, b):
    bsz, seq, w = zc.shape
    padded = seq + 2 * D_HALO
    spec = pl.BlockSpec((1, seq, w), lambda bi: (bi, 0, 0))
    return pl.pallas_call(
        functools.partial(_conv_kernel, seq=seq),
        grid=(bsz,),
        in_specs=[spec, spec, spec, _resident(wc.shape), _resident(wd.shape),
                  _resident((1, w)), _resident((1, w))],
        out_specs=[spec, spec],
        out_shape=[jax.ShapeDtypeStruct(zc.shape, BF16)] * 2,
        scratch_shapes=[pltpu.VMEM((padded, w), F32),
                        pltpu.VMEM((seq, w), F32)],
        compiler_params=_params("parallel"),
        name="conv",
    )(zc, cb, zd, wc, wd, g, b)


def _rope_tables(seq):
    t = jnp.arange(seq)
    row = (t // GRID_W).astype(F32)
    col = (t % GRID_W).astype(F32)
    half = HEAD_DIM // 2
    freqs = ROPE_THETA ** (-jnp.arange(0, half, 2, dtype=F32) / half)
    ang = jnp.concatenate([row[:, None] * freqs, col[:, None] * freqs], axis=-1)
    cos = jnp.repeat(jnp.cos(ang), 2, axis=-1)
    sin = jnp.repeat(jnp.sin(ang), 2, axis=-1)
    sign = jnp.tile(jnp.array([-1.0, 1.0], F32), half)
    return jnp.tile(cos, (1, 2)), jnp.tile(sin * sign, (1, 2))


def kernel(x, ffn_norm, mix_norm, ffn_w_gate, ffn_w_up, ffn_w_down, ab_w_in, ab_w_out,
           a_q_norm, a_k_norm, b_rpb, cd_w_in, cd_w_out, c_conv_w, d_conv_w,
           d_norm_g, d_norm_b, final_norm):
    bsz, seq, d = x.shape
    assert d == D_MODEL and seq % TOKEN_TILE == 0 and seq % GRID_W == 0
    n = bsz * seq
    rows = seq // GRID_W
    cos_t, sin_t = _rope_tables(seq)
    seg = jnp.asarray(np.tile(np.kron(np.eye(LANES // HEAD_DIM), np.ones((HEAD_DIM, HEAD_DIM))), (2, 1)), BF16)
    row_vec = lambda v: v.reshape(1, -1).astype(F32)

    def ffn(xf, i, k, mix=None):
        last = i == DEPTH - 1 and k == 1
        return _ffn(xf, row_vec(ffn_norm[i, k]), ffn_w_gate[i, k].astype(BF16),
                    ffn_w_up[i, k].astype(BF16), ffn_w_down[i, k].astype(BF16),
                    row_vec(final_norm), final_norm=last, mix=mix)

    xf = x.reshape(n, d)
    for i in range(DEPTH):
        xf = ffn(xf, i, 0)
        j = i // 2
        if i % 2 == 0:
            two = lambda g: row_vec(jnp.tile(g, LANES // HEAD_DIM))
            aq, akd, avd, bq, bk, bv = _ab_in(
                xf, row_vec(mix_norm[i]), ab_w_in[j].astype(BF16), two(a_q_norm[j]),
                two(a_k_norm[j]), cos_t, sin_t, seg, seq)
            r3 = lambda a: a.reshape(bsz, seq, a.shape[1])
            ya = _gqa(r3(aq), r3(akd), r3(avd))
            yb = _na(r3(bq), r3(bk), r3(bv), _na_bias(b_rpb[j], rows))
            mix = (ya.reshape(n, A_Q), yb.reshape(n, B_W), ab_w_out[j].astype(BF16))
        else:
            zc, cb, zd = _cd_in(xf, row_vec(mix_norm[i]), cd_w_in[j].astype(BF16))
            r3 = lambda a: a.reshape(bsz, seq, a.shape[1])
            yc, yd = _conv(r3(zc), r3(cb), r3(zd), c_conv_w[j].astype(F32), d_conv_w[j].astype(F32),
                           row_vec(d_norm_g[j]), row_vec(d_norm_b[j]))
            mix = (yc.reshape(n, C_WIDTH), yd.reshape(n, D_WIDTH), cd_w_out[j].astype(BF16))
        xf = ffn(xf, i, 1, mix)
    return xf.reshape(bsz, seq, d)
```

```python
import functools

import jax
import jax.numpy as jnp
import numpy as np
from jax import lax
from jax.experimental import pallas as pl
from jax.experimental.pallas import tpu as pltpu

D_MODEL = 1024
DEPTH = 4
HEAD_DIM = 64
A_HEADS = 8
A_KV_HEADS = 2
B_HEADS = 8
C_WIDTH = 512
D_WIDTH = 512
C_CONV = 3
D_CONV = 31
D_FF = 2816
GRID_W = 64
NA_ROWS = 8
NA_COLS = 16
ROPE_THETA = 10000.0
EPS = 1e-6
NEG_INF = -1e30
A_Q = A_HEADS * HEAD_DIM
A_KV = A_KV_HEADS * HEAD_DIM
B_W = B_HEADS * HEAD_DIM

LANES = 128
VMEM_LIMIT_BYTES = 56 * 1024 * 1024
TOKEN_TILE = 512
FF_CHUNK = 256
Q_TILE = 1024
Q_CHAIN = 512
NA_ROWS_PER_STEP = 8
CONV_ROWS = 128
D_HALO = 16

F32 = jnp.float32
BF16 = jnp.bfloat16
Q_SCALE = HEAD_DIM ** -0.5 * float(np.log2(np.e))


def _params(*sem):
    return pltpu.CompilerParams(dimension_semantics=sem, vmem_limit_bytes=VMEM_LIMIT_BYTES)


def _resident(shape):
    zeros = (0,) * len(shape)
    return pl.BlockSpec(shape, lambda *_: zeros, pipeline_mode=pl.Buffered(1))


def _rms_rows(x, gain):
    ms = jnp.mean(x * x, axis=-1, keepdims=True)
    return x * lax.rsqrt(ms + EPS) * gain


def _ffn_kernel(*refs, final_norm, mixed):
    if mixed:
        x_ref, ya_ref, yb_ref, wo_ref, nw_ref, wg_ref, wu_ref, wd_ref, fw_ref, o_ref, h_ref, acc_ref, x1_ref = refs
        half = ya_ref.shape[1]
        x1_ref[...] = (x_ref[...]
                       + jnp.dot(ya_ref[...], wo_ref[:half], preferred_element_type=F32)
                       + jnp.dot(yb_ref[...], wo_ref[half:], preferred_element_type=F32))
    else:
        x_ref, nw_ref, wg_ref, wu_ref, wd_ref, fw_ref, o_ref, h_ref, acc_ref = refs
        x1_ref = x_ref
    h_ref[...] = _rms_rows(x1_ref[...], nw_ref[...]).astype(BF16)
    nch = D_FF // FF_CHUNK
    for c in range(nch):
        cols = slice(c * FF_CHUNK, (c + 1) * FF_CHUNK)
        h = h_ref[...]
        g = jnp.dot(h, wg_ref[:, cols], preferred_element_type=F32)
        u = jnp.dot(h, wu_ref[:, cols], preferred_element_type=F32)
        a = (g * jax.nn.sigmoid(g) * u).astype(BF16)
        d = jnp.dot(a, wd_ref[cols, :], preferred_element_type=F32)
        if c == 0:
            acc_ref[...] = d
        elif c < nch - 1:
            acc_ref[...] += d
        else:
            y = x1_ref[...] + 0.5 * (acc_ref[...] + d)
            o_ref[...] = _rms_rows(y, fw_ref[...]) if final_norm else y


def _ffn(x, nw, wg, wu, wd, fw, final_norm, mix=None):
    n = x.shape[0]
    tile = lambda w: pl.BlockSpec((TOKEN_TILE, w), lambda i: (i, 0))
    ffn_specs = [_resident((1, D_MODEL)), _resident(wg.shape), _resident(wu.shape),
                 _resident(wd.shape), _resident((1, D_MODEL))]
    scratch = [pltpu.VMEM((TOKEN_TILE, D_MODEL), BF16), pltpu.VMEM((TOKEN_TILE, D_MODEL), F32)]
    if mix is None:
        operands, mix_specs = (x,), [tile(D_MODEL)]
    else:
        ya, yb, wo = mix
        operands = (x, ya, yb, wo)
        mix_specs = [tile(D_MODEL), tile(ya.shape[1]), tile(yb.shape[1]), _resident(wo.shape)]
        scratch.append(pltpu.VMEM((TOKEN_TILE, D_MODEL), F32))
    return pl.pallas_call(
        functools.partial(_ffn_kernel, final_norm=final_norm, mixed=mix is not None),
        grid=(n // TOKEN_TILE,),
        in_specs=mix_specs + ffn_specs,
        out_specs=tile(D_MODEL),
        out_shape=jax.ShapeDtypeStruct(x.shape, F32),
        scratch_shapes=scratch,
        compiler_params=_params("parallel"),
        name="ffn_mix" if mix is not None else "ffn",
    )(*operands, nw, wg, wu, wd, fw)


def _swap_pairs(x):
    lane = lax.broadcasted_iota(jnp.int32, x.shape, 1)
    return jnp.where(lane % 2 == 0, pltpu.roll(x, LANES - 1, 1), pltpu.roll(x, 1, 1))


def _head_norm_rope(xg, gain, cos, sin_signed, seg_ones):
    sq = xg * xg
    hi = sq.astype(BF16)
    lo = (sq - hi.astype(F32)).astype(BF16)
    ssum = jnp.dot(jnp.concatenate([hi, lo], axis=1), seg_ones, preferred_element_type=F32)
    xn = xg * lax.rsqrt(ssum * (1.0 / HEAD_DIM) + EPS) * gain
    return xn * cos + _swap_pairs(xn) * sin_signed


def _dup_halves(x):
    lane = lax.broadcasted_iota(jnp.int32, x.shape, 1)
    rolled = pltpu.roll(x, HEAD_DIM, 1)
    low = lane < HEAD_DIM
    return jnp.where(low, x, rolled), jnp.where(low, rolled, x)


def _ab_in_kernel(x_ref, nw_ref, w_ref, qg_ref, kg_ref, cos_ref, sin_ref, seg_ref,
                  aq_ref, ak_ref, av_ref, bq_ref, bk_ref, bv_ref, h_ref):
    h_ref[...] = _rms_rows(x_ref[...], nw_ref[...]).astype(BF16)
    h = h_ref[...]
    cos = cos_ref[...]
    sin = sin_ref[...]
    seg = seg_ref[...]

    def proj(lo, width):
        return jnp.dot(h, w_ref[:, lo:lo + width], preferred_element_type=F32)

    aq = proj(0, A_Q)
    for j in range(A_Q // LANES):
        q = _head_norm_rope(aq[:, j * LANES:(j + 1) * LANES], qg_ref[...], cos, sin, seg)
        aq_ref[:, j * LANES:(j + 1) * LANES] = (q * Q_SCALE).astype(BF16)
    akv = proj(A_Q, 2 * A_KV)
    k0, k1 = _dup_halves(_head_norm_rope(akv[:, :A_KV], kg_ref[...], cos, sin, seg))
    ak_ref[:, :LANES] = k0.astype(BF16)
    ak_ref[:, LANES:] = k1.astype(BF16)
    v0, v1 = _dup_halves(akv[:, A_KV:])
    av_ref[:, :LANES] = v0.astype(BF16)
    av_ref[:, LANES:] = v1.astype(BF16)
    base = A_Q + 2 * A_KV
    bq_ref[...] = (proj(base, B_W) * Q_SCALE).astype(BF16)
    bk_ref[...] = proj(base + B_W, B_W).astype(BF16)
    bv_ref[...] = proj(base + 2 * B_W, B_W).astype(BF16)


def _ab_in(x, nw, w_in, qg, kg, cos_t, sin_t, seg, seq):
    n = x.shape[0]
    ab_in = w_in.shape[1]
    tiles_per_seq = seq // TOKEN_TILE
    tile = lambda w: pl.BlockSpec((TOKEN_TILE, w), lambda i: (i, 0))
    rope = pl.BlockSpec((TOKEN_TILE, LANES), lambda i: (i % tiles_per_seq, 0))
    widths = (A_Q, 2 * LANES, 2 * LANES, B_W, B_W, B_W)
    return pl.pallas_call(
        _ab_in_kernel,
        grid=(n // TOKEN_TILE,),
        in_specs=[tile(D_MODEL), _resident((1, D_MODEL)), _resident((D_MODEL, ab_in)),
                  _resident((1, LANES)), _resident((1, LANES)), rope, rope,
                  _resident((2 * LANES, LANES))],
        out_specs=[tile(w) for w in widths],
        out_shape=[jax.ShapeDtypeStruct((n, w), BF16) for w in widths],
        scratch_shapes=[pltpu.VMEM((TOKEN_TILE, D_MODEL), BF16)],
        compiler_params=_params("parallel"),
        name="ab_in",
    )(x, nw, w_in, qg, kg, cos_t, sin_t, seg)


def _stack_heads(qp):
    lane = lax.broadcasted_iota(jnp.int32, qp.shape, 1)
    zero = jnp.zeros_like(qp)
    return jnp.concatenate([jnp.where(lane < HEAD_DIM, qp, zero),
                            jnp.where(lane < HEAD_DIM, zero, qp)], axis=0)


def _unstack_heads(o, rows):
    lane = lax.broadcasted_iota(jnp.int32, (rows, LANES), 1)
    return jnp.where(lane < HEAD_DIM, o[:rows], o[rows:])


def _softmax_pv(s, v):
    m = jnp.max(s, axis=-1, keepdims=True)
    p = jnp.exp2(s - m)
    l = jnp.sum(p, axis=-1, keepdims=True)
    return jnp.dot(p.astype(BF16), v, preferred_element_type=F32) / l


_NT = (((1,), (1,)), ((), ()))


def _gqa_kernel(q_ref, k_ref, v_ref, o_ref):
    k = k_ref[0]
    v = v_ref[0]
    lane = lax.broadcasted_iota(jnp.int32, (Q_CHAIN, LANES), 1)
    masks = (lane < HEAD_DIM, lane >= HEAD_DIM)
    chains = [(blk, h) for blk in range(Q_TILE // Q_CHAIN) for h in range(A_HEADS // A_KV_HEADS)]

    def block(c):
        blk, h = chains[c]
        return slice(blk * Q_CHAIN, (blk + 1) * Q_CHAIN), slice((h // 2) * LANES, (h // 2 + 1) * LANES)

    def scores(c):
        rows, cols = block(c)
        qp = q_ref[0, rows, cols]
        qh = jnp.where(masks[chains[c][1] % 2], qp, jnp.zeros_like(qp))
        return lax.dot_general(qh, k, _NT, preferred_element_type=F32)

    s_next = scores(0)
    outs = []
    for c in range(len(chains)):
        s_cur = s_next
        if c + 1 < len(chains):
            s_next = scores(c + 1)
        outs.append(_softmax_pv(s_cur, v))
        if c % 2 == 1:
            rows, cols = block(c)
            o_ref[0, rows, cols] = jnp.where(lane < HEAD_DIM, outs[c - 1], outs[c]).astype(BF16)


def _gqa(aq, akd, avd):
    b, s, _ = aq.shape
    qspec = pl.BlockSpec((1, Q_TILE, 2 * LANES), lambda bi, g, qi: (bi, qi, g))
    kvspec = pl.BlockSpec((1, s, LANES), lambda bi, g, qi: (bi, 0, g))
    return pl.pallas_call(
        _gqa_kernel,
        grid=(b, A_KV_HEADS, s // Q_TILE),
        in_specs=[qspec, kvspec, kvspec],
        out_specs=qspec,
        out_shape=jax.ShapeDtypeStruct(aq.shape, BF16),
        compiler_params=_params("parallel", "parallel", "arbitrary"),
        name="gqa",
    )(aq, akd, avd)


def _na_kernel(q_ref, k_ref, v_ref, bias_ref, o_ref, *, rows):
    band = NA_ROWS * GRID_W
    pairs = B_W // LANES
    contract_rows = (((0,), (0,)), ((), ()))

    def row_group(i, carry):
        chains = []
        for rr in range(NA_ROWS_PER_STEP):
            r = i * NA_ROWS_PER_STEP + rr
            rs = jnp.clip(r - NA_ROWS // 2, 0, rows - NA_ROWS)
            q0 = pl.multiple_of(r * GRID_W, GRID_W)
            k0 = pl.multiple_of(rs * GRID_W, GRID_W)
            chains += [(r - rs, q0, k0, j) for j in range(pairs)]

        def scores(c):
            d, q0, k0, j = chains[c]
            cols = slice(j * LANES, (j + 1) * LANES)
            qs = _stack_heads(q_ref[0, pl.ds(q0, GRID_W), cols])
            kb = k_ref[0, pl.ds(k0, band), cols]
            return lax.dot_general(kb, qs, _NT, preferred_element_type=F32) + bias_ref[d, j]

        s_next = scores(0)
        for c in range(len(chains)):
            s_cur = s_next
            if c + 1 < len(chains):
                s_next = scores(c + 1)
            d, q0, k0, j = chains[c]
            cols = slice(j * LANES, (j + 1) * LANES)
            m = jnp.max(s_cur, axis=0, keepdims=True)
            p = jnp.exp2(s_cur - m)
            l = jnp.sum(p, axis=0, keepdims=True)
            pn = (p * (1.0 / l)).astype(BF16)
            o = lax.dot_general(pn, v_ref[0, pl.ds(k0, band), cols], contract_rows, preferred_element_type=F32)
            o_ref[0, pl.ds(q0, GRID_W), cols] = _unstack_heads(o, GRID_W).astype(BF16)
        return carry

    lax.fori_loop(0, rows // NA_ROWS_PER_STEP, row_group, 0)


def _na(bq, bk, bv, bias):
    b, s, w = bq.shape
    rows = s // GRID_W
    spec = pl.BlockSpec((1, s, w), lambda bi: (bi, 0, 0))
    return pl.pallas_call(
        functools.partial(_na_kernel, rows=rows),
        grid=(b,),
        in_specs=[spec, spec, spec, _resident(bias.shape)],
        out_specs=spec,
        out_shape=jax.ShapeDtypeStruct(bq.shape, BF16),
        compiler_params=_params("parallel"),
        name="na",
    )(bq, bk, bv, bias)


def _na_bias(rpb, rows):
    wr = min(NA_ROWS, rows)
    qc = np.arange(GRID_W)[:, None]
    kc = np.arange(GRID_W)[None, :]
    col_off = np.clip(kc - qc, -(NA_COLS - 1), NA_COLS - 1) + NA_COLS - 1
    wcs = np.clip(qc - NA_COLS // 2, 0, GRID_W - NA_COLS)
    valid = (kc >= wcs) & (kc < wcs + NA_COLS)
    onehot = (col_off[None] == np.arange(2 * NA_COLS - 1)[:, None, None]).astype(np.float32)
    by_col = jnp.einsum("hrc,cqk->hrqk", rpb.astype(F32), jnp.asarray(onehot),
                        precision=lax.Precision.HIGHEST)
    t = jnp.stack([by_col[:, NA_ROWS - 1 - d:NA_ROWS - 1 - d + wr] for d in range(wr)])
    t = jnp.where(valid[None, None, None], t * float(np.log2(np.e)), NEG_INF)
    t = t.reshape(wr, B_HEADS // 2, 2, wr, GRID_W, GRID_W)
    t = t.transpose(0, 1, 3, 5, 2, 4)
    return t.reshape(wr, B_HEADS // 2, wr * GRID_W, 2 * GRID_W)


def _cd_in_kernel(x_ref, nw_ref, w_ref, zc_ref, cb_ref, zd_ref, h_ref):
    h_ref[...] = _rms_rows(x_ref[...], nw_ref[...]).astype(BF16)
    h = h_ref[...]

    def proj(i):
        return jnp.dot(h, w_ref[:, i * C_WIDTH:(i + 1) * C_WIDTH], preferred_element_type=F32)

    zc_ref[...] = (proj(2) * proj(0)).astype(BF16)
    cb_ref[...] = proj(1).astype(BF16)
    zd_ref[...] = (proj(3) * jax.nn.sigmoid(proj(4))).astype(BF16)


def _cd_in(x, nw, w_in):
    n = x.shape[0]
    tile = lambda w: pl.BlockSpec((TOKEN_TILE, w), lambda i: (i, 0))
    return pl.pallas_call(
        _cd_in_kernel,
        grid=(n // TOKEN_TILE,),
        in_specs=[tile(D_MODEL), _resident((1, D_MODEL)), _resident(w_in.shape)],
        out_specs=[tile(C_WIDTH)] * 3,
        out_shape=[jax.ShapeDtypeStruct((n, C_WIDTH), BF16)] * 3,
        scratch_shapes=[pltpu.VMEM((TOKEN_TILE, D_MODEL), BF16)],
        compiler_params=_params("parallel"),
        name="cd_in",
    )(x, nw, w_in)


def _conv_kernel(zc_ref, cb_ref, zd_ref, wc_ref, wd_ref, g_ref, b_ref, yc_ref, yd_ref,
                 pad_ref, raw_ref, *, seq):
    halo = jnp.zeros((D_HALO, C_WIDTH), F32)
    nchunks = seq // CONV_ROWS
    window = CONV_ROWS + 2 * D_HALO

    def conv(src_ref, w_ref, taps, emit):
        pad_ref[:D_HALO] = halo
        pad_ref[D_HALO + seq:] = halo
        pad_ref[D_HALO:D_HALO + seq] = src_ref[0].astype(F32)
        first = D_HALO - taps // 2
        for c in range(C_WIDTH // LANES):
            cols = slice(c * LANES, (c + 1) * LANES)
            w = w_ref[:, cols]

            def chunk(i, carry):
                r0 = pl.multiple_of(i * CONV_ROWS, CONV_ROWS)
                win = pad_ref[pl.ds(r0, window), cols]
                acc = jnp.zeros((CONV_ROWS, LANES), F32)
                for sh in range(8):
                    offs = [first + k for k in range(taps) if (first + k) % 8 == sh]
                    if not offs:
                        continue
                    shifted = win if sh == 0 else pltpu.roll(win, window - sh, 0)
                    for off in offs:
                        k = off - first
                        base = off - sh
                        acc = acc + w[k:k + 1, :] * shifted[base:base + CONV_ROWS]
                emit(r0, cols, acc)
                return carry

            lax.fori_loop(0, nchunks, chunk, 0, unroll=4)

    def emit_c(r0, cols, acc):
        yc_ref[0, pl.ds(r0, CONV_ROWS), cols] = (
            cb_ref[0, pl.ds(r0, CONV_ROWS), cols].astype(F32) * acc).astype(BF16)

    def emit_d(r0, cols, acc):
        raw_ref[pl.ds(r0, CONV_ROWS), cols] = acc

    conv(zc_ref, wc_ref, C_CONV, emit_c)
    conv(zd_ref, wd_ref, D_CONV, emit_d)

    def norm(i, carry):
        r0 = pl.multiple_of(i * CONV_ROWS, CONV_ROWS)
        y = raw_ref[pl.ds(r0, CONV_ROWS), :]
        mu = jnp.mean(y, axis=-1, keepdims=True)
        var = jnp.mean(jnp.square(y - mu), axis=-1, keepdims=True)
        z = (y - mu) * lax.rsqrt(var + EPS) * g_ref[...] + b_ref[...]
        yd_ref[0, pl.ds(r0, CONV_ROWS), :] = (z * jax.nn.sigmoid(z)).astype(BF16)
        return carry

    lax.fori_loop(0, nchunks, norm, 0, unroll=4)


def _conv(zc, cb, zd, wc, wd, g, b):
    bsz, seq, w = zc.shape
    padded = seq + 2 * D_HALO
    spec = pl.BlockSpec((1, seq, w), lambda bi: (bi, 0, 0))
    return pl.pallas_call(
        functools.partial(_conv_kernel, seq=seq),
        grid=(bsz,),
        in_specs=[spec, spec, spec, _resident(wc.shape), _resident(wd.shape),
                  _resident((1, w)), _resident((1, w))],
        out_specs=[spec, spec],
        out_shape=[jax.ShapeDtypeStruct(zc.shape, BF16)] * 2,
        scratch_shapes=[pltpu.VMEM((padded, w), F32),
                        pltpu.VMEM((seq, w), F32)],
        compiler_params=_params("parallel"),
        name="conv",
    )(zc, cb, zd, wc, wd, g, b)


def _rope_tables(seq):
    t = jnp.arange(seq)
    row = (t // GRID_W).astype(F32)
    col = (t % GRID_W).astype(F32)
    half = HEAD_DIM // 2
    freqs = ROPE_THETA ** (-jnp.arange(0, half, 2, dtype=F32) / half)
    ang = jnp.concatenate([row[:, None] * freqs, col[:, None] * freqs], axis=-1)
    cos = jnp.repeat(jnp.cos(ang), 2, axis=-1)
    sin = jnp.repeat(jnp.sin(ang), 2, axis=-1)
    sign = jnp.tile(jnp.array([-1.0, 1.0], F32), half)
    return jnp.tile(cos, (1, 2)), jnp.tile(sin * sign, (1, 2))


def kernel(x, ffn_norm, mix_norm, ffn_w_gate, ffn_w_up, ffn_w_down, ab_w_in, ab_w_out,
           a_q_norm, a_k_norm, b_rpb, cd_w_in, cd_w_out, c_conv_w, d_conv_w,
           d_norm_g, d_norm_b, final_norm):
    bsz, seq, d = x.shape
    assert d == D_MODEL and seq % TOKEN_TILE == 0 and seq % GRID_W == 0
    n = bsz * seq
    rows = seq // GRID_W
    cos_t, sin_t = _rope_tables(seq)
    seg = jnp.asarray(np.tile(np.kron(np.eye(LANES // HEAD_DIM), np.ones((HEAD_DIM, HEAD_DIM))), (2, 1)), BF16)
    row_vec = lambda v: v.reshape(1, -1).astype(F32)

    def ffn(xf, i, k, mix=None):
        last = i == DEPTH - 1 and k == 1
        return _ffn(xf, row_vec(ffn_norm[i, k]), ffn_w_gate[i, k].astype(BF16),
                    ffn_w_up[i, k].astype(BF16), ffn_w_down[i, k].astype(BF16),
                    row_vec(final_norm), final_norm=last, mix=mix)

    xf = x.reshape(n, d)
    for i in range(DEPTH):
        xf = ffn(xf, i, 0)
        j = i // 2
        if i % 2 == 0:
            two = lambda g: row_vec(jnp.tile(g, LANES // HEAD_DIM))
            aq, akd, avd, bq, bk, bv = _ab_in(
                xf, row_vec(mix_norm[i]), ab_w_in[j].astype(BF16), two(a_q_norm[j]),
                two(a_k_norm[j]), cos_t, sin_t, seg, seq)
            r3 = lambda a: a.reshape(bsz, seq, a.shape[1])
            ya = _gqa(r3(aq), r3(akd), r3(avd))
            yb = _na(r3(bq), r3(bk), r3(bv), _na_bias(b_rpb[j], rows))
            mix = (ya.reshape(n, A_Q), yb.reshape(n, B_W), ab_w_out[j].astype(BF16))
        else:
            zc, cb, zd = _cd_in(xf, row_vec(mix_norm[i]), cd_w_in[j].astype(BF16))
            r3 = lambda a: a.reshape(bsz, seq, a.shape[1])
            yc, yd = _conv(r3(zc), r3(cb), r3(zd), c_conv_w[j].astype(F32), d_conv_w[j].astype(F32),
                           row_vec(d_norm_g[j]), row_vec(d_norm_b[j]))
            mix = (yc.reshape(n, C_WIDTH), yd.reshape(n, D_WIDTH), cd_w_out[j].astype(BF16))
        xf = ffn(xf, i, 1, mix)
    return xf.reshape(bsz, seq, d)
```

```python
import functools

import jax
import jax.numpy as jnp
import numpy as np
from jax import lax
from jax.experimental import pallas as pl
from jax.experimental.pallas import tpu as pltpu

D_MODEL = 1024
DEPTH = 4
HEAD_DIM = 64
A_HEADS = 8
A_KV_HEADS = 2
B_HEADS = 8
C_WIDTH = 512
D_WIDTH = 512
C_CONV = 3
D_CONV = 31
D_FF = 2816
GRID_W = 64
NA_ROWS = 8
NA_COLS = 16
ROPE_THETA = 10000.0
EPS = 1e-6
NEG_INF = -1e30
A_Q = A_HEADS * HEAD_DIM
A_KV = A_KV_HEADS * HEAD_DIM
B_W = B_HEADS * HEAD_DIM

LANES = 128
VMEM_LIMIT_BYTES = 56 * 1024 * 1024
TOKEN_TILE = 512
FFN_TILE = 1024
FFN_SUB_TILE = 512
FF_CHUNK = 256
Q_TILE = 1024
Q_CHAIN = 512
NA_ROWS_PER_STEP = 8
CONV_ROWS = 128
D_HALO = 16

F32 = jnp.float32
BF16 = jnp.bfloat16
Q_SCALE = HEAD_DIM ** -0.5 * float(np.log2(np.e))


def _params(*sem):
    return pltpu.CompilerParams(dimension_semantics=sem, vmem_limit_bytes=VMEM_LIMIT_BYTES)


def _resident(shape):
    zeros = (0,) * len(shape)
    return pl.BlockSpec(shape, lambda *_: zeros, pipeline_mode=pl.Buffered(1))


def _rms_rows(x, gain):
    ms = jnp.mean(x * x, axis=-1, keepdims=True)
    return x * lax.rsqrt(ms + EPS) * gain


def _ffn_kernel(*refs, final_norm, mixed):
    if mixed:
        x_ref, ya_ref, yb_ref, wo_ref, nw_ref, wg_ref, wu_ref, wd_ref, fw_ref, o_ref, h_ref = refs
        half = ya_ref.shape[1]
    else:
        x_ref, nw_ref, wg_ref, wu_ref, wd_ref, fw_ref, o_ref, h_ref = refs
    nch = D_FF // FF_CHUNK
    subs = [slice(t * FFN_SUB_TILE, (t + 1) * FFN_SUB_TILE) for t in range(FFN_TILE // FFN_SUB_TILE)]
    for rows in subs:
        x1 = x_ref[rows, :]
        if mixed:
            x1 = (x1 + jnp.dot(ya_ref[rows, :], wo_ref[:half], preferred_element_type=F32)
                  + jnp.dot(yb_ref[rows, :], wo_ref[half:], preferred_element_type=F32))
        h_ref[rows, :] = _rms_rows(x1, nw_ref[...]).astype(BF16)
        o_ref[rows, :] = x1
    for rows in subs:
        for c in range(nch):
            cols = slice(c * FF_CHUNK, (c + 1) * FF_CHUNK)
            h = h_ref[rows, :]
            g = jnp.dot(h, wg_ref[:, cols], preferred_element_type=F32)
            u = jnp.dot(h, wu_ref[:, cols], preferred_element_type=F32)
            a = (g * jax.nn.sigmoid(g) * u).astype(BF16)
            o_ref[rows, :] += 0.5 * jnp.dot(a, wd_ref[cols, :], preferred_element_type=F32)
        if final_norm:
            o_ref[rows, :] = _rms_rows(o_ref[rows, :], fw_ref[...])


def _ffn(x, nw, wg, wu, wd, fw, final_norm, mix=None):
    n = x.shape[0]
    tile = lambda w: pl.BlockSpec((FFN_TILE, w), lambda i: (i, 0))
    ffn_specs = [_resident((1, D_MODEL)), _resident(wg.shape), _resident(wu.shape),
                 _resident(wd.shape), _resident((1, D_MODEL))]
    if mix is None:
        operands, mix_specs = (x,), [tile(D_MODEL)]
    else:
        ya, yb, wo = mix
        operands = (x, ya, yb, wo)
        mix_specs = [tile(D_MODEL), tile(ya.shape[1]), tile(yb.shape[1]), _resident(wo.shape)]
    return pl.pallas_call(
        functools.partial(_ffn_kernel, final_norm=final_norm, mixed=mix is not None),
        grid=(n // FFN_TILE,),
        in_specs=mix_specs + ffn_specs,
        out_specs=tile(D_MODEL),
        out_shape=jax.ShapeDtypeStruct(x.shape, F32),
        scratch_shapes=[pltpu.VMEM((FFN_TILE, D_MODEL), BF16)],
        compiler_params=_params("parallel"),
        name="ffn_mix" if mix is not None else "ffn",
    )(*operands, nw, wg, wu, wd, fw)


def _swap_pairs(x):
    lane = lax.broadcasted_iota(jnp.int32, x.shape, 1)
    return jnp.where(lane % 2 == 0, pltpu.roll(x, LANES - 1, 1), pltpu.roll(x, 1, 1))


def _head_norm_rope(xg, gain, cos, sin_signed, seg_ones):
    sq = xg * xg
    hi = sq.astype(BF16)
    lo = (sq - hi.astype(F32)).astype(BF16)
    ssum = jnp.dot(jnp.concatenate([hi, lo], axis=1), seg_ones, preferred_element_type=F32)
    xn = xg * lax.rsqrt(ssum * (1.0 / HEAD_DIM) + EPS) * gain
    return xn * cos + _swap_pairs(xn) * sin_signed


def _dup_halves(x):
    lane = lax.broadcasted_iota(jnp.int32, x.shape, 1)
    rolled = pltpu.roll(x, HEAD_DIM, 1)
    low = lane < HEAD_DIM
    return jnp.where(low, x, rolled), jnp.where(low, rolled, x)


def _ab_in_kernel(x_ref, nw_ref, w_ref, qg_ref, kg_ref, cos_ref, sin_ref, seg_ref,
                  aq_ref, ak_ref, av_ref, bq_ref, bk_ref, bv_ref, h_ref):
    h_ref[...] = _rms_rows(x_ref[...], nw_ref[...]).astype(BF16)
    h = h_ref[...]
    cos = cos_ref[...]
    sin = sin_ref[...]
    seg = seg_ref[...]

    def proj(lo, width):
        return jnp.dot(h, w_ref[:, lo:lo + width], preferred_element_type=F32)

    aq = proj(0, A_Q)
    for j in range(A_Q // LANES):
        q = _head_norm_rope(aq[:, j * LANES:(j + 1) * LANES], qg_ref[...], cos, sin, seg)
        aq_ref[:, j * LANES:(j + 1) * LANES] = (q * Q_SCALE).astype(BF16)
    akv = proj(A_Q, 2 * A_KV)
    k0, k1 = _dup_halves(_head_norm_rope(akv[:, :A_KV], kg_ref[...], cos, sin, seg))
    ak_ref[:, :LANES] = k0.astype(BF16)
    ak_ref[:, LANES:] = k1.astype(BF16)
    v0, v1 = _dup_halves(akv[:, A_KV:])
    av_ref[:, :LANES] = v0.astype(BF16)
    av_ref[:, LANES:] = v1.astype(BF16)
    base = A_Q + 2 * A_KV
    bq_ref[...] = (proj(base, B_W) * Q_SCALE).astype(BF16)
    bk_ref[...] = proj(base + B_W, B_W).astype(BF16)
    bv_ref[...] = proj(base + 2 * B_W, B_W).astype(BF16)


def _ab_in(x, nw, w_in, qg, kg, cos_t, sin_t, seg, seq):
    n = x.shape[0]
    ab_in = w_in.shape[1]
    tiles_per_seq = seq // TOKEN_TILE
    tile = lambda w: pl.BlockSpec((TOKEN_TILE, w), lambda i: (i, 0))
    rope = pl.BlockSpec((TOKEN_TILE, LANES), lambda i: (i % tiles_per_seq, 0))
    widths = (A_Q, 2 * LANES, 2 * LANES, B_W, B_W, B_W)
    return pl.pallas_call(
        _ab_in_kernel,
        grid=(n // TOKEN_TILE,),
        in_specs=[tile(D_MODEL), _resident((1, D_MODEL)), _resident((D_MODEL, ab_in)),
                  _resident((1, LANES)), _resident((1, LANES)), rope, rope,
                  _resident((2 * LANES, LANES))],
        out_specs=[tile(w) for w in widths],
        out_shape=[jax.ShapeDtypeStruct((n, w), BF16) for w in widths],
        scratch_shapes=[pltpu.VMEM((TOKEN_TILE, D_MODEL), BF16)],
        compiler_params=_params("parallel"),
        name="ab_in",
    )(x, nw, w_in, qg, kg, cos_t, sin_t, seg)


def _stack_heads(qp):
    lane = lax.broadcasted_iota(jnp.int32, qp.shape, 1)
    zero = jnp.zeros_like(qp)
    return jnp.concatenate([jnp.where(lane < HEAD_DIM, qp, zero),
                            jnp.where(lane < HEAD_DIM, zero, qp)], axis=0)


def _unstack_heads(o, rows):
    lane = lax.broadcasted_iota(jnp.int32, (rows, LANES), 1)
    return jnp.where(lane < HEAD_DIM, o[:rows], o[rows:])


def _softmax_pv(s, v):
    m = jnp.max(s, axis=-1, keepdims=True)
    p = jnp.exp2(s - m)
    l = jnp.sum(p, axis=-1, keepdims=True)
    return jnp.dot(p.astype(BF16), v, preferred_element_type=F32) / l


_NT = (((1,), (1,)), ((), ()))


def _gqa_kernel(q_ref, k_ref, v_ref, o_ref):
    k = k_ref[0]
    v = v_ref[0]
    lane = lax.broadcasted_iota(jnp.int32, (Q_CHAIN, LANES), 1)
    masks = (lane < HEAD_DIM, lane >= HEAD_DIM)
    chains = [(blk, h) for blk in range(Q_TILE // Q_CHAIN) for h in range(A_HEADS // A_KV_HEADS)]

    def block(c):
        blk, h = chains[c]
        return slice(blk * Q_CHAIN, (blk + 1) * Q_CHAIN), slice((h // 2) * LANES, (h // 2 + 1) * LANES)

    def scores(c):
        rows, cols = block(c)
        qp = q_ref[0, rows, cols]
        qh = jnp.where(masks[chains[c][1] % 2], qp, jnp.zeros_like(qp))
        return lax.dot_general(qh, k, _NT, preferred_element_type=F32)

    s_next = scores(0)
    outs = []
    for c in range(len(chains)):
        s_cur = s_next
        if c + 1 < len(chains):
            s_next = scores(c + 1)
        outs.append(_softmax_pv(s_cur, v))
        if c % 2 == 1:
            rows, cols = block(c)
            o_ref[0, rows, cols] = jnp.where(lane < HEAD_DIM, outs[c - 1], outs[c]).astype(BF16)


def _gqa(aq, akd, avd):
    b, s, _ = aq.shape
    qspec = pl.BlockSpec((1, Q_TILE, 2 * LANES), lambda bi, g, qi: (bi, qi, g))
    kvspec = pl.BlockSpec((1, s, LANES), lambda bi, g, qi: (bi, 0, g))
    return pl.pallas_call(
        _gqa_kernel,
        grid=(b, A_KV_HEADS, s // Q_TILE),
        in_specs=[qspec, kvspec, kvspec],
        out_specs=qspec,
        out_shape=jax.ShapeDtypeStruct(aq.shape, BF16),
        compiler_params=_params("parallel", "parallel", "arbitrary"),
        name="gqa",
    )(aq, akd, avd)


def _na_kernel(q_ref, k_ref, v_ref, bias_ref, o_ref, *, rows):
    band = NA_ROWS * GRID_W
    pairs = B_W // LANES
    contract_rows = (((0,), (0,)), ((), ()))

    def row_group(i, carry):
        chains = []
        for rr in range(NA_ROWS_PER_STEP):
            r = i * NA_ROWS_PER_STEP + rr
            rs = jnp.clip(r - NA_ROWS // 2, 0, rows - NA_ROWS)
            q0 = pl.multiple_of(r * GRID_W, GRID_W)
            k0 = pl.multiple_of(rs * GRID_W, GRID_W)
            chains += [(r - rs, q0, k0, j) for j in range(pairs)]

        def scores(c):
            d, q0, k0, j = chains[c]
            cols = slice(j * LANES, (j + 1) * LANES)
            qs = _stack_heads(q_ref[0, pl.ds(q0, GRID_W), cols])
            kb = k_ref[0, pl.ds(k0, band), cols]
            return lax.dot_general(kb, qs, _NT, preferred_element_type=F32) + bias_ref[d, j]

        s_next = scores(0)
        for c in range(len(chains)):
            s_cur = s_next
            if c + 1 < len(chains):
                s_next = scores(c + 1)
            d, q0, k0, j = chains[c]
            cols = slice(j * LANES, (j + 1) * LANES)
            m = jnp.max(s_cur, axis=0, keepdims=True)
            p = jnp.exp2(s_cur - m)
            l = jnp.sum(p, axis=0, keepdims=True)
            pn = (p * (1.0 / l)).astype(BF16)
            o = lax.dot_general(pn, v_ref[0, pl.ds(k0, band), cols], contract_rows, preferred_element_type=F32)
            o_ref[0, pl.ds(q0, GRID_W), cols] = _unstack_heads(o, GRID_W).astype(BF16)
        return carry

    lax.fori_loop(0, rows // NA_ROWS_PER_STEP, row_group, 0)


def _na(bq, bk, bv, bias):
    b, s, w = bq.shape
    rows = s // GRID_W
    spec = pl.BlockSpec((1, s, w), lambda bi: (bi, 0, 0))
    return pl.pallas_call(
        functools.partial(_na_kernel, rows=rows),
        grid=(b,),
        in_specs=[spec, spec, spec, _resident(bias.shape)],
        out_specs=spec,
        out_shape=jax.ShapeDtypeStruct(bq.shape, BF16),
        compiler_params=_params("parallel"),
        name="na",
    )(bq, bk, bv, bias)


def _na_bias(rpb, rows):
    wr = min(NA_ROWS, rows)
    qc = np.arange(GRID_W)[:, None]
    kc = np.arange(GRID_W)[None, :]
    col_off = np.clip(kc - qc, -(NA_COLS - 1), NA_COLS - 1) + NA_COLS - 1
    wcs = np.clip(qc - NA_COLS // 2, 0, GRID_W - NA_COLS)
    valid = (kc >= wcs) & (kc < wcs + NA_COLS)
    onehot = (col_off[None] == np.arange(2 * NA_COLS - 1)[:, None, None]).astype(np.float32)
    by_col = jnp.einsum("hrc,cqk->hrqk", rpb.astype(F32), jnp.asarray(onehot),
                        precision=lax.Precision.HIGHEST)
    t = jnp.stack([by_col[:, NA_ROWS - 1 - d:NA_ROWS - 1 - d + wr] for d in range(wr)])
    t = jnp.where(valid[None, None, None], t * float(np.log2(np.e)), NEG_INF)
    t = t.reshape(wr, B_HEADS // 2, 2, wr, GRID_W, GRID_W)
    t = t.transpose(0, 1, 3, 5, 2, 4)
    return t.reshape(wr, B_HEADS // 2, wr * GRID_W, 2 * GRID_W)


def _cd_in_kernel(x_ref, nw_ref, w_ref, zc_ref, cb_ref, zd_ref, h_ref):
    h_ref[...] = _rms_rows(x_ref[...], nw_ref[...]).astype(BF16)
    h = h_ref[...]

    def proj(i):
        return jnp.dot(h, w_ref[:, i * C_WIDTH:(i + 1) * C_WIDTH], preferred_element_type=F32)

    zc_ref[...] = (proj(2) * proj(0)).astype(BF16)
    cb_ref[...] = proj(1).astype(BF16)
    zd_ref[...] = (proj(3) * jax.nn.sigmoid(proj(4))).astype(BF16)


def _cd_in(x, nw, w_in):
    n = x.shape[0]
    tile = lambda w: pl.BlockSpec((TOKEN_TILE, w), lambda i: (i, 0))
    return pl.pallas_call(
        _cd_in_kernel,
        grid=(n // TOKEN_TILE,),
        in_specs=[tile(D_MODEL), _resident((1, D_MODEL)), _resident(w_in.shape)],
        out_specs=[tile(C_WIDTH)] * 3,
        out_shape=[jax.ShapeDtypeStruct((n, C_WIDTH), BF16)] * 3,
        scratch_shapes=[pltpu.VMEM((TOKEN_TILE, D_MODEL), BF16)],
        compiler_params=_params("parallel"),
        name="cd_in",
    )(x, nw, w_in)


def _conv_kernel(zc_ref, cb_ref, zd_ref, wc_ref, wd_ref, g_ref, b_ref, yc_ref, yd_ref,
                 pad_ref, raw_ref, *, seq):
    halo = jnp.zeros((D_HALO, C_WIDTH), F32)
    nchunks = seq // CONV_ROWS
    window = CONV_ROWS + 2 * D_HALO

    def conv(src_ref, w_ref, taps, emit):
        pad_ref[:D_HALO] = halo
        pad_ref[D_HALO + seq:] = halo
        pad_ref[D_HALO:D_HALO + seq] = src_ref[0].astype(F32)
        first = D_HALO - taps // 2
        for c in range(C_WIDTH // LANES):
            cols = slice(c * LANES, (c + 1) * LANES)
            w = w_ref[:, cols]

            def chunk(i, carry):
                r0 = pl.multiple_of(i * CONV_ROWS, CONV_ROWS)
                win = pad_ref[pl.ds(r0, window), cols]
                acc = jnp.zeros((CONV_ROWS, LANES), F32)
                for sh in range(8):
                    offs = [first + k for k in range(taps) if (first + k) % 8 == sh]
                    if not offs:
                        continue
                    shifted = win if sh == 0 else pltpu.roll(win, window - sh, 0)
                    for off in offs:
                        k = off - first
                        base = off - sh
                        acc = acc + w[k:k + 1, :] * shifted[base:base + CONV_ROWS]
                emit(r0, cols, acc)
                return carry

            lax.fori_loop(0, nchunks, chunk, 0, unroll=4)

    def emit_c(r0, cols, acc):
        yc_ref[0, pl.ds(r0, CONV_ROWS), cols] = (
            cb_ref[0, pl.ds(r0, CONV_ROWS), cols].astype(F32) * acc).astype(BF16)

    def emit_d(r0, cols, acc):
        raw_ref[pl.ds(r0, CONV_ROWS), cols] = acc

    conv(zc_ref, wc_ref, C_CONV, emit_c)
    conv(zd_ref, wd_ref, D_CONV, emit_d)

    def norm(i, carry):
        r0 = pl.multiple_of(i * CONV_ROWS, CONV_ROWS)
        y = raw_ref[pl.ds(r0, CONV_ROWS), :]
        mu = jnp.mean(y, axis=-1, keepdims=True)
        var = jnp.mean(jnp.square(y - mu), axis=-1, keepdims=True)
        z = (y - mu) * lax.rsqrt(var + EPS) * g_ref[...] + b_ref[...]
        yd_ref[0, pl.ds(r0, CONV_ROWS), :] = (z * jax.nn.sigmoid(z)).astype(BF16)
        return carry

    lax.fori_loop(0, nchunks, norm, 0, unroll=4)


def _conv(zc, cb, zd, wc, wd, g, b):
    bsz, seq, w = zc.shape
    padded = seq + 2 * D_HALO
    spec = pl.BlockSpec((1, seq, w), lambda bi: (bi, 0, 0))
    return pl.pallas_call(
        functools.partial(_conv_kernel, seq=seq),
        grid=(bsz,),
        in_specs=[spec, spec, spec, _resident(wc.shape), _resident(wd.shape),
                  _resident((1, w)), _resident((1, w))],
        out_specs=[spec, spec],
        out_shape=[jax.ShapeDtypeStruct(zc.shape, BF16)] * 2,
        scratch_shapes=[pltpu.VMEM((padded, w), F32),
                        pltpu.VMEM((seq, w), F32)],
        compiler_params=_params("parallel"),
        name="conv",
    )(zc, cb, zd, wc, wd, g, b)


def _rope_tables(seq):
    t = jnp.arange(seq)
    row = (t // GRID_W).astype(F32)
    col = (t % GRID_W).astype(F32)
    half = HEAD_DIM // 2
    freqs = ROPE_THETA ** (-jnp.arange(0, half, 2, dtype=F32) / half)
    ang = jnp.concatenate([row[:, None] * freqs, col[:, None] * freqs], axis=-1)
    cos = jnp.repeat(jnp.cos(ang), 2, axis=-1)
    sin = jnp.repeat(jnp.sin(ang), 2, axis=-1)
    sign = jnp.tile(jnp.array([-1.0, 1.0], F32), half)
    return jnp.tile(cos, (1, 2)), jnp.tile(sin * sign, (1, 2))


def kernel(x, ffn_norm, mix_norm, ffn_w_gate, ffn_w_up, ffn_w_down, ab_w_in, ab_w_out,
           a_q_norm, a_k_norm, b_rpb, cd_w_in, cd_w_out, c_conv_w, d_conv_w,
           d_norm_g, d_norm_b, final_norm):
    bsz, seq, d = x.shape
    assert d == D_MODEL and seq % TOKEN_TILE == 0 and seq % GRID_W == 0
    n = bsz * seq
    rows = seq // GRID_W
    cos_t, sin_t = _rope_tables(seq)
    seg = jnp.asarray(np.tile(np.kron(np.eye(LANES // HEAD_DIM), np.ones((HEAD_DIM, HEAD_DIM))), (2, 1)), BF16)
    row_vec = lambda v: v.reshape(1, -1).astype(F32)

    def ffn(xf, i, k, mix=None):
        last = i == DEPTH - 1 and k == 1
        return _ffn(xf, row_vec(ffn_norm[i, k]), ffn_w_gate[i, k].astype(BF16),
                    ffn_w_up[i, k].astype(BF16), ffn_w_down[i, k].astype(BF16),
                    row_vec(final_norm), final_norm=last, mix=mix)

    xf = x.reshape(n, d)
    for i in range(DEPTH):
        xf = ffn(xf, i, 0)
        j = i // 2
        if i % 2 == 0:
            two = lambda g: row_vec(jnp.tile(g, LANES // HEAD_DIM))
            aq, akd, avd, bq, bk, bv = _ab_in(
                xf, row_vec(mix_norm[i]), ab_w_in[j].astype(BF16), two(a_q_norm[j]),
                two(a_k_norm[j]), cos_t, sin_t, seg, seq)
            r3 = lambda a: a.reshape(bsz, seq, a.shape[1])
            ya = _gqa(r3(aq), r3(akd), r3(avd))
            yb = _na(r3(bq), r3(bk), r3(bv), _na_bias(b_rpb[j], rows))
            mix = (ya.reshape(n, A_Q), yb.reshape(n, B_W), ab_w_out[j].astype(BF16))
        else:
            zc, cb, zd = _cd_in(xf, row_vec(mix_norm[i]), cd_w_in[j].astype(BF16))
            r3 = lambda a: a.reshape(bsz, seq, a.shape[1])
            yc, yd = _conv(r3(zc), r3(cb), r3(zd), c_conv_w[j].astype(F32), d_conv_w[j].astype(F32),
                           row_vec(d_norm_g[j]), row_vec(d_norm_b[j]))
            mix = (yc.reshape(n, C_WIDTH), yd.reshape(n, D_WIDTH), cd_w_out[j].astype(BF16))
        xf = ffn(xf, i, 1, mix)
    return xf.reshape(bsz, seq, d)
```

```python
import functools

import jax
import jax.numpy as jnp
import numpy as np
from jax import lax
from jax.experimental import pallas as pl
from jax.experimental.pallas import tpu as pltpu

D_MODEL = 1024
DEPTH = 4
HEAD_DIM = 64
A_HEADS = 8
A_KV_HEADS = 2
B_HEADS = 8
C_WIDTH = 512
D_WIDTH = 512
C_CONV = 3
D_CONV = 31
D_FF = 2816
GRID_W = 64
NA_ROWS = 8
NA_COLS = 16
ROPE_THETA = 10000.0
EPS = 1e-6
NEG_INF = -1e30
A_Q = A_HEADS * HEAD_DIM
A_KV = A_KV_HEADS * HEAD_DIM
B_W = B_HEADS * HEAD_DIM

LANES = 128
VMEM_LIMIT_BYTES = 56 * 1024 * 1024
TOKEN_TILE = 512
FFN_TILE = 1024
FFN_SUB_TILE = 512
FF_CHUNK = 256
Q_TILE = 1024
Q_CHAIN = 512
Q_AHEAD = 3
NA_ROWS_PER_STEP = 8
CONV_ROWS = 128
D_HALO = 16

F32 = jnp.float32
BF16 = jnp.bfloat16
Q_SCALE = HEAD_DIM ** -0.5 * float(np.log2(np.e))


def _params(*sem):
    return pltpu.CompilerParams(dimension_semantics=sem, vmem_limit_bytes=VMEM_LIMIT_BYTES)


def _resident(shape):
    zeros = (0,) * len(shape)
    return pl.BlockSpec(shape, lambda *_: zeros, pipeline_mode=pl.Buffered(1))


def _rms_rows(x, gain):
    ms = jnp.mean(x * x, axis=-1, keepdims=True)
    return x * lax.rsqrt(ms + EPS) * gain


def _ffn_kernel(*refs, final_norm, mixed):
    if mixed:
        x_ref, ya_ref, yb_ref, wo_ref, nw_ref, wg_ref, wu_ref, wd_ref, fw_ref, o_ref, h_ref = refs
        half = ya_ref.shape[1]
    else:
        x_ref, nw_ref, wg_ref, wu_ref, wd_ref, fw_ref, o_ref, h_ref = refs
    nch = D_FF // FF_CHUNK
    subs = [slice(t * FFN_SUB_TILE, (t + 1) * FFN_SUB_TILE) for t in range(FFN_TILE // FFN_SUB_TILE)]
    for rows in subs:
        x1 = x_ref[rows, :]
        if mixed:
            x1 = (x1 + jnp.dot(ya_ref[rows, :], wo_ref[:half], preferred_element_type=F32)
                  + jnp.dot(yb_ref[rows, :], wo_ref[half:], preferred_element_type=F32))
        h_ref[rows, :] = _rms_rows(x1, nw_ref[...]).astype(BF16)
        o_ref[rows, :] = x1
    for rows in subs:
        for c in range(nch):
            cols = slice(c * FF_CHUNK, (c + 1) * FF_CHUNK)
            h = h_ref[rows, :]
            g = jnp.dot(h, wg_ref[:, cols], preferred_element_type=F32)
            u = jnp.dot(h, wu_ref[:, cols], preferred_element_type=F32)
            a = (g * jax.nn.sigmoid(g) * u).astype(BF16)
            o_ref[rows, :] += 0.5 * jnp.dot(a, wd_ref[cols, :], preferred_element_type=F32)
        if final_norm:
            o_ref[rows, :] = _rms_rows(o_ref[rows, :], fw_ref[...])


def _ffn(x, nw, wg, wu, wd, fw, final_norm, mix=None):
    n = x.shape[0]
    tile = lambda w: pl.BlockSpec((FFN_TILE, w), lambda i: (i, 0))
    ffn_specs = [_resident((1, D_MODEL)), _resident(wg.shape), _resident(wu.shape),
                 _resident(wd.shape), _resident((1, D_MODEL))]
    if mix is None:
        operands, mix_specs = (x,), [tile(D_MODEL)]
    else:
        ya, yb, wo = mix
        operands = (x, ya, yb, wo)
        mix_specs = [tile(D_MODEL), tile(ya.shape[1]), tile(yb.shape[1]), _resident(wo.shape)]
    return pl.pallas_call(
        functools.partial(_ffn_kernel, final_norm=final_norm, mixed=mix is not None),
        grid=(n // FFN_TILE,),
        in_specs=mix_specs + ffn_specs,
        out_specs=tile(D_MODEL),
        out_shape=jax.ShapeDtypeStruct(x.shape, F32),
        scratch_shapes=[pltpu.VMEM((FFN_TILE, D_MODEL), BF16)],
        compiler_params=_params("parallel"),
        name="ffn_mix" if mix is not None else "ffn",
    )(*operands, nw, wg, wu, wd, fw)


def _swap_pairs(x):
    lane = lax.broadcasted_iota(jnp.int32, x.shape, 1)
    return jnp.where(lane % 2 == 0, pltpu.roll(x, LANES - 1, 1), pltpu.roll(x, 1, 1))


def _head_norm_rope(xg, gain, cos, sin_signed, seg_ones):
    sq = xg * xg
    hi = sq.astype(BF16)
    lo = (sq - hi.astype(F32)).astype(BF16)
    ssum = jnp.dot(jnp.concatenate([hi, lo], axis=1), seg_ones, preferred_element_type=F32)
    xn = xg * lax.rsqrt(ssum * (1.0 / HEAD_DIM) + EPS) * gain
    return xn * cos + _swap_pairs(xn) * sin_signed


def _dup_halves(x):
    lane = lax.broadcasted_iota(jnp.int32, x.shape, 1)
    rolled = pltpu.roll(x, HEAD_DIM, 1)
    low = lane < HEAD_DIM
    return jnp.where(low, x, rolled), jnp.where(low, rolled, x)


def _ab_in_kernel(x_ref, nw_ref, w_ref, qg_ref, kg_ref, cos_ref, sin_ref, seg_ref,
                  aq_ref, ak_ref, avt_ref, bq_ref, bk_ref, bv_ref, h_ref):
    h_ref[...] = _rms_rows(x_ref[...], nw_ref[...]).astype(BF16)
    h = h_ref[...]
    cos = cos_ref[...]
    sin = sin_ref[...]
    seg = seg_ref[...]

    def proj(lo, width):
        return jnp.dot(h, w_ref[:, lo:lo + width], preferred_element_type=F32)

    aq = proj(0, A_Q)
    for j in range(A_Q // LANES):
        q = _head_norm_rope(aq[:, j * LANES:(j + 1) * LANES], qg_ref[...], cos, sin, seg)
        aq_ref[:, j * LANES:(j + 1) * LANES] = (q * Q_SCALE).astype(BF16)
    akv = proj(A_Q, 2 * A_KV)
    k0, k1 = _dup_halves(_head_norm_rope(akv[:, :A_KV], kg_ref[...], cos, sin, seg))
    ak_ref[:, :LANES] = k0.astype(BF16)
    ak_ref[:, LANES:] = k1.astype(BF16)
    avt_ref[0] = akv[:, A_KV:].T.astype(BF16)
    base = A_Q + 2 * A_KV
    bq_ref[...] = (proj(base, B_W) * Q_SCALE).astype(BF16)
    bk_ref[...] = proj(base + B_W, B_W).astype(BF16)
    bv_ref[...] = proj(base + 2 * B_W, B_W).astype(BF16)


def _ab_in(x, nw, w_in, qg, kg, cos_t, sin_t, seg, seq):
    n = x.shape[0]
    ab_in = w_in.shape[1]
    tiles_per_seq = seq // TOKEN_TILE
    tile = lambda w: pl.BlockSpec((TOKEN_TILE, w), lambda i: (i, 0))
    rope = pl.BlockSpec((TOKEN_TILE, LANES), lambda i: (i % tiles_per_seq, 0))
    rows_out = lambda w: (tile(w), jax.ShapeDtypeStruct((n, w), BF16))
    vt_out = (pl.BlockSpec((1, A_KV, TOKEN_TILE), lambda i: (i // tiles_per_seq, 0, i % tiles_per_seq)),
              jax.ShapeDtypeStruct((n // seq, A_KV, seq), BF16))
    outs = [rows_out(A_Q), rows_out(2 * LANES), vt_out, rows_out(B_W), rows_out(B_W), rows_out(B_W)]
    return pl.pallas_call(
        _ab_in_kernel,
        grid=(n // TOKEN_TILE,),
        in_specs=[tile(D_MODEL), _resident((1, D_MODEL)), _resident((D_MODEL, ab_in)),
                  _resident((1, LANES)), _resident((1, LANES)), rope, rope,
                  _resident((2 * LANES, LANES))],
        out_specs=[o[0] for o in outs],
        out_shape=[o[1] for o in outs],
        scratch_shapes=[pltpu.VMEM((TOKEN_TILE, D_MODEL), BF16)],
        compiler_params=_params("parallel"),
        name="ab_in",
    )(x, nw, w_in, qg, kg, cos_t, sin_t, seg)


def _stack_heads(qp):
    lane = lax.broadcasted_iota(jnp.int32, qp.shape, 1)
    zero = jnp.zeros_like(qp)
    return jnp.concatenate([jnp.where(lane < HEAD_DIM, qp, zero),
                            jnp.where(lane < HEAD_DIM, zero, qp)], axis=0)


def _unstack_heads(o, rows):
    lane = lax.broadcasted_iota(jnp.int32, (rows, LANES), 1)
    return jnp.where(lane < HEAD_DIM, o[:rows], o[rows:])


_NT = (((1,), (1,)), ((), ()))


def _gqa_kernel(q_ref, k_ref, vt_ref, o_ref):
    k = k_ref[0]
    vt = vt_ref[0]
    lane = lax.broadcasted_iota(jnp.int32, (Q_CHAIN, LANES), 1)
    masks = (lane < HEAD_DIM, lane >= HEAD_DIM)
    chains = [(blk, h) for blk in range(Q_TILE // Q_CHAIN) for h in range(A_HEADS // A_KV_HEADS)]

    def block(c):
        blk, h = chains[c]
        return slice(blk * Q_CHAIN, (blk + 1) * Q_CHAIN), slice((h // 2) * LANES, (h // 2 + 1) * LANES)

    def scores(c):
        rows, cols = block(c)
        qp = q_ref[0, rows, cols]
        qh = jnp.where(masks[chains[c][1] % 2], qp, jnp.zeros_like(qp))
        return lax.dot_general(k, qh, _NT, preferred_element_type=F32)

    def attend(st):
        m = jnp.max(st, axis=0, keepdims=True)
        p = jnp.exp2(st - m)
        l = jnp.sum(p, axis=0, keepdims=True)
        return jnp.dot(vt, p.astype(BF16), preferred_element_type=F32) / l

    pending = [scores(c) for c in range(Q_AHEAD)]
    outs = []
    for c in range(len(chains)):
        if c + Q_AHEAD < len(chains):
            pending.append(scores(c + Q_AHEAD))
        outs.append(attend(pending[c]))
        pending[c] = None
        if c % 2 == 1:
            rows, cols = block(c)
            o_ref[0, rows, cols] = jnp.concatenate([outs[c - 1], outs[c]], axis=0).T.astype(BF16)


def _gqa(aq, akd, avt):
    b, s, _ = aq.shape
    qspec = pl.BlockSpec((1, Q_TILE, 2 * LANES), lambda bi, g, qi: (bi, qi, g))
    kspec = pl.BlockSpec((1, s, LANES), lambda bi, g, qi: (bi, 0, g))
    vspec = pl.BlockSpec((1, HEAD_DIM, s), lambda bi, g, qi: (bi, g, 0))
    return pl.pallas_call(
        _gqa_kernel,
        grid=(b, A_KV_HEADS, s // Q_TILE),
        in_specs=[qspec, kspec, vspec],
        out_specs=qspec,
        out_shape=jax.ShapeDtypeStruct(aq.shape, BF16),
        compiler_params=_params("parallel", "parallel", "arbitrary"),
        name="gqa",
    )(aq, akd, avt)


def _na_kernel(q_ref, k_ref, v_ref, bias_ref, o_ref, *, rows):
    band = NA_ROWS * GRID_W
    pairs = B_W // LANES
    contract_rows = (((0,), (0,)), ((), ()))

    def row_group(i, carry):
        chains = []
        for rr in range(NA_ROWS_PER_STEP):
            r = i * NA_ROWS_PER_STEP + rr
            rs = jnp.clip(r - NA_ROWS // 2, 0, rows - NA_ROWS)
            q0 = pl.multiple_of(r * GRID_W, GRID_W)
            k0 = pl.multiple_of(rs * GRID_W, GRID_W)
            chains += [(r - rs, q0, k0, j) for j in range(pairs)]

        def scores(c):
            d, q0, k0, j = chains[c]
            cols = slice(j * LANES, (j + 1) * LANES)
            qs = _stack_heads(q_ref[0, pl.ds(q0, GRID_W), cols])
            kb = k_ref[0, pl.ds(k0, band), cols]
            return lax.dot_general(kb, qs, _NT, preferred_element_type=F32) + bias_ref[d, j]

        s_next = scores(0)
        for c in range(len(chains)):
            s_cur = s_next
            if c + 1 < len(chains):
                s_next = scores(c + 1)
            d, q0, k0, j = chains[c]
            cols = slice(j * LANES, (j + 1) * LANES)
            m = jnp.max(s_cur, axis=0, keepdims=True)
            p = jnp.exp2(s_cur - m)
            l = jnp.sum(p, axis=0, keepdims=True)
            pn = (p * (1.0 / l)).astype(BF16)
            o = lax.dot_general(pn, v_ref[0, pl.ds(k0, band), cols], contract_rows, preferred_element_type=F32)
            o_ref[0, pl.ds(q0, GRID_W), cols] = _unstack_heads(o, GRID_W).astype(BF16)
        return carry

    lax.fori_loop(0, rows // NA_ROWS_PER_STEP, row_group, 0)


def _na(bq, bk, bv, bias):
    b, s, w = bq.shape
    rows = s // GRID_W
    spec = pl.BlockSpec((1, s, w), lambda bi: (bi, 0, 0))
    return pl.pallas_call(
        functools.partial(_na_kernel, rows=rows),
        grid=(b,),
        in_specs=[spec, spec, spec, _resident(bias.shape)],
        out_specs=spec,
        out_shape=jax.ShapeDtypeStruct(bq.shape, BF16),
        compiler_params=_params("parallel"),
        name="na",
    )(bq, bk, bv, bias)


def _na_bias(rpb, rows):
    wr = min(NA_ROWS, rows)
    qc = np.arange(GRID_W)[:, None]
    kc = np.arange(GRID_W)[None, :]
    col_off = np.clip(kc - qc, -(NA_COLS - 1), NA_COLS - 1) + NA_COLS - 1
    wcs = np.clip(qc - NA_COLS // 2, 0, GRID_W - NA_COLS)
    valid = (kc >= wcs) & (kc < wcs + NA_COLS)
    onehot = (col_off[None] == np.arange(2 * NA_COLS - 1)[:, None, None]).astype(np.float32)
    by_col = jnp.einsum("hrc,cqk->hrqk", rpb.astype(F32), jnp.asarray(onehot),
                        precision=lax.Precision.HIGHEST)
    t = jnp.stack([by_col[:, NA_ROWS - 1 - d:NA_ROWS - 1 - d + wr] for d in range(wr)])
    t = jnp.where(valid[None, None, None], t * float(np.log2(np.e)), NEG_INF)
    t = t.reshape(wr, B_HEADS // 2, 2, wr, GRID_W, GRID_W)
    t = t.transpose(0, 1, 3, 5, 2, 4)
    return t.reshape(wr, B_HEADS // 2, wr * GRID_W, 2 * GRID_W)


def _cd_in_kernel(x_ref, nw_ref, w_ref, zc_ref, cb_ref, zd_ref, h_ref):
    h_ref[...] = _rms_rows(x_ref[...], nw_ref[...]).astype(BF16)
    h = h_ref[...]

    def proj(i):
        return jnp.dot(h, w_ref[:, i * C_WIDTH:(i + 1) * C_WIDTH], preferred_element_type=F32)

    zc_ref[...] = (proj(2) * proj(0)).astype(BF16)
    cb_ref[...] = proj(1).astype(BF16)
    zd_ref[...] = (proj(3) * jax.nn.sigmoid(proj(4))).astype(BF16)


def _cd_in(x, nw, w_in):
    n = x.shape[0]
    tile = lambda w: pl.BlockSpec((TOKEN_TILE, w), lambda i: (i, 0))
    return pl.pallas_call(
        _cd_in_kernel,
        grid=(n // TOKEN_TILE,),
        in_specs=[tile(D_MODEL), _resident((1, D_MODEL)), _resident(w_in.shape)],
        out_specs=[tile(C_WIDTH)] * 3,
        out_shape=[jax.ShapeDtypeStruct((n, C_WIDTH), BF16)] * 3,
        scratch_shapes=[pltpu.VMEM((TOKEN_TILE, D_MODEL), BF16)],
        compiler_params=_params("parallel"),
        name="cd_in",
    )(x, nw, w_in)


def _conv_kernel(zc_ref, cb_ref, zd_ref, wc_ref, wd_ref, g_ref, b_ref, yc_ref, yd_ref,
                 pad_ref, raw_ref, *, seq):
    halo = jnp.zeros((D_HALO, C_WIDTH), F32)
    nchunks = seq // CONV_ROWS
    window = CONV_ROWS + 2 * D_HALO

    def conv(src_ref, w_ref, taps, emit):
        pad_ref[:D_HALO] = halo
        pad_ref[D_HALO + seq:] = halo
        pad_ref[D_HALO:D_HALO + seq] = src_ref[0].astype(F32)
        first = D_HALO - taps // 2
        for c in range(C_WIDTH // LANES):
            cols = slice(c * LANES, (c + 1) * LANES)
            w = w_ref[:, cols]

            def chunk(i, carry):
                r0 = pl.multiple_of(i * CONV_ROWS, CONV_ROWS)
                win = pad_ref[pl.ds(r0, window), cols]
                acc = jnp.zeros((CONV_ROWS, LANES), F32)
                for sh in range(8):
                    offs = [first + k for k in range(taps) if (first + k) % 8 == sh]
                    if not offs:
                        continue
                    shifted = win if sh == 0 else pltpu.roll(win, window - sh, 0)
                    for off in offs:
                        k = off - first
                        base = off - sh
                        acc = acc + w[k:k + 1, :] * shifted[base:base + CONV_ROWS]
                emit(r0, cols, acc)
                return carry

            lax.fori_loop(0, nchunks, chunk, 0, unroll=4)

    def emit_c(r0, cols, acc):
        yc_ref[0, pl.ds(r0, CONV_ROWS), cols] = (
            cb_ref[0, pl.ds(r0, CONV_ROWS), cols].astype(F32) * acc).astype(BF16)

    def emit_d(r0, cols, acc):
        raw_ref[pl.ds(r0, CONV_ROWS), cols] = acc

    conv(zc_ref, wc_ref, C_CONV, emit_c)
    conv(zd_ref, wd_ref, D_CONV, emit_d)

    def norm(i, carry):
        r0 = pl.multiple_of(i * CONV_ROWS, CONV_ROWS)
        y = raw_ref[pl.ds(r0, CONV_ROWS), :]
        mu = jnp.mean(y, axis=-1, keepdims=True)
        var = jnp.mean(jnp.square(y - mu), axis=-1, keepdims=True)
        z = (y - mu) * lax.rsqrt(var + EPS) * g_ref[...] + b_ref[...]
        yd_ref[0, pl.ds(r0, CONV_ROWS), :] = (z * jax.nn.sigmoid(z)).astype(BF16)
        return carry

    lax.fori_loop(0, nchunks, norm, 0, unroll=4)


def _conv(zc, cb, zd, wc, wd, g, b):
    bsz, seq, w = zc.shape
    padded = seq + 2 * D_HALO
    spec = pl.BlockSpec((1, seq, w), lambda bi: (bi, 0, 0))
    return pl.pallas_call(
        functools.partial(_conv_kernel, seq=seq),
        grid=(bsz,),
        in_specs=[spec, spec, spec, _resident(wc.shape), _resident(wd.shape),
                  _resident((1, w)), _resident((1, w))],
        out_specs=[spec, spec],
        out_shape=[jax.ShapeDtypeStruct(zc.shape, BF16)] * 2,
        scratch_shapes=[pltpu.VMEM((padded, w), F32),
                        pltpu.VMEM((seq, w), F32)],
        compiler_params=_params("parallel"),
        name="conv",
    )(zc, cb, zd, wc, wd, g, b)


def _rope_tables(seq):
    t = jnp.arange(seq)
    row = (t // GRID_W).astype(F32)
    col = (t % GRID_W).astype(F32)
    half = HEAD_DIM // 2
    freqs = ROPE_THETA ** (-jnp.arange(0, half, 2, dtype=F32) / half)
    ang = jnp.concatenate([row[:, None] * freqs, col[:, None] * freqs], axis=-1)
    cos = jnp.repeat(jnp.cos(ang), 2, axis=-1)
    sin = jnp.repeat(jnp.sin(ang), 2, axis=-1)
    sign = jnp.tile(jnp.array([-1.0, 1.0], F32), half)
    return jnp.tile(cos, (1, 2)), jnp.tile(sin * sign, (1, 2))


def kernel(x, ffn_norm, mix_norm, ffn_w_gate, ffn_w_up, ffn_w_down, ab_w_in, ab_w_out,
           a_q_norm, a_k_norm, b_rpb, cd_w_in, cd_w_out, c_conv_w, d_conv_w,
           d_norm_g, d_norm_b, final_norm):
    bsz, seq, d = x.shape
    assert d == D_MODEL and seq % TOKEN_TILE == 0 and seq % GRID_W == 0
    n = bsz * seq
    rows = seq // GRID_W
    cos_t, sin_t = _rope_tables(seq)
    seg = jnp.asarray(np.tile(np.kron(np.eye(LANES // HEAD_DIM), np.ones((HEAD_DIM, HEAD_DIM))), (2, 1)), BF16)
    row_vec = lambda v: v.reshape(1, -1).astype(F32)

    def ffn(xf, i, k, mix=None):
        last = i == DEPTH - 1 and k == 1
        return _ffn(xf, row_vec(ffn_norm[i, k]), ffn_w_gate[i, k].astype(BF16),
                    ffn_w_up[i, k].astype(BF16), ffn_w_down[i, k].astype(BF16),
                    row_vec(final_norm), final_norm=last, mix=mix)

    xf = x.reshape(n, d)
    for i in range(DEPTH):
        xf = ffn(xf, i, 0)
        j = i // 2
        if i % 2 == 0:
            two = lambda g: row_vec(jnp.tile(g, LANES // HEAD_DIM))
            aq, akd, avt, bq, bk, bv = _ab_in(
                xf, row_vec(mix_norm[i]), ab_w_in[j].astype(BF16), two(a_q_norm[j]),
                two(a_k_norm[j]), cos_t, sin_t, seg, seq)
            r3 = lambda a: a.reshape(bsz, seq, a.shape[1])
            ya = _gqa(r3(aq), r3(akd), avt)
            yb = _na(r3(bq), r3(bk), r3(bv), _na_bias(b_rpb[j], rows))
            mix = (ya.reshape(n, A_Q), yb.reshape(n, B_W), ab_w_out[j].astype(BF16))
        else:
            zc, cb, zd = _cd_in(xf, row_vec(mix_norm[i]), cd_w_in[j].astype(BF16))
            r3 = lambda a: a.reshape(bsz, seq, a.shape[1])
            yc, yd = _conv(r3(zc), r3(cb), r3(zd), c_conv_w[j].astype(F32), d_conv_w[j].astype(F32),
                           row_vec(d_norm_g[j]), row_vec(d_norm_b[j]))
            mix = (yc.reshape(n, C_WIDTH), yd.reshape(n, D_WIDTH), cd_w_out[j].astype(BF16))
        xf = ffn(xf, i, 1, mix)
    return xf.reshape(bsz, seq, d)
```

```python
import functools

import jax
import jax.numpy as jnp
import numpy as np
from jax import lax
from jax.experimental import pallas as pl
from jax.experimental.pallas import tpu as pltpu

D_MODEL = 1024
DEPTH = 4
HEAD_DIM = 64
A_HEADS = 8
A_KV_HEADS = 2
B_HEADS = 8
C_WIDTH = 512
D_WIDTH = 512
C_CONV = 3
D_CONV = 31
D_FF = 2816
GRID_W = 64
NA_ROWS = 8
NA_COLS = 16
ROPE_THETA = 10000.0
EPS = 1e-6
NEG_INF = -1e30
A_Q = A_HEADS * HEAD_DIM
A_KV = A_KV_HEADS * HEAD_DIM
B_W = B_HEADS * HEAD_DIM

LANES = 128
SUBLANES = 8
VMEM_LIMIT_BYTES = 56 * 1024 * 1024
TOKEN_TILE = 512
FFN_TILE = 1024
FFN_SUB_TILE = 512
FF_CHUNK = 256
Q_TILE = 1024
Q_CHAIN = 512
NA_ROWS_PER_STEP = 8
CONV_ROWS = 128
D_HALO = 16

F32 = jnp.float32
BF16 = jnp.bfloat16
Q_SCALE = HEAD_DIM ** -0.5 * float(np.log2(np.e))


def _params(*sem):
    return pltpu.CompilerParams(dimension_semantics=sem, vmem_limit_bytes=VMEM_LIMIT_BYTES)


def _resident(shape):
    zeros = (0,) * len(shape)
    return pl.BlockSpec(shape, lambda *_: zeros, pipeline_mode=pl.Buffered(1))


def _resident_slab(stacked, index):
    tail = stacked.shape[len(index):]
    at = tuple(index) + (0,) * len(tail)
    return pl.BlockSpec((None,) * len(index) + tail, lambda *_: at, pipeline_mode=pl.Buffered(1))


def _rms_rows(x, gain):
    ms = jnp.mean(x * x, axis=-1, keepdims=True)
    return x * lax.rsqrt(ms + EPS) * gain


def _ffn_kernel(*refs, final_norm, mixed):
    if mixed:
        x_ref, ya_ref, yb_ref, wo_ref, nw_ref, wg_ref, wu_ref, wd_ref, fw_ref, o_ref, h_ref = refs
        half = ya_ref.shape[1]
    else:
        x_ref, nw_ref, wg_ref, wu_ref, wd_ref, fw_ref, o_ref, h_ref = refs
    nch = D_FF // FF_CHUNK
    subs = [slice(t * FFN_SUB_TILE, (t + 1) * FFN_SUB_TILE) for t in range(FFN_TILE // FFN_SUB_TILE)]
    for rows in subs:
        x1 = x_ref[rows, :]
        if mixed:
            x1 = (x1 + jnp.dot(ya_ref[rows, :], wo_ref[:half], preferred_element_type=F32)
                  + jnp.dot(yb_ref[rows, :], wo_ref[half:], preferred_element_type=F32))
        h_ref[rows, :] = _rms_rows(x1, nw_ref[...]).astype(BF16)
        o_ref[rows, :] = x1
    for rows in subs:
        for c in range(nch):
            cols = slice(c * FF_CHUNK, (c + 1) * FF_CHUNK)
            h = h_ref[rows, :]
            g = jnp.dot(h, wg_ref[:, cols], preferred_element_type=F32)
            u = jnp.dot(h, wu_ref[:, cols], preferred_element_type=F32)
            a = (g * jax.nn.sigmoid(g) * u).astype(BF16)
            o_ref[rows, :] += 0.5 * jnp.dot(a, wd_ref[cols, :], preferred_element_type=F32)
        if final_norm:
            o_ref[rows, :] = _rms_rows(o_ref[rows, :], fw_ref[...])


def _ffn(x, nw, wg, wu, wd, layer, fw, final_norm, mix=None):
    n = x.shape[0]
    tile = lambda w: pl.BlockSpec((FFN_TILE, w), lambda i: (i, 0))
    ffn_specs = [_resident((1, D_MODEL)), _resident_slab(wg, layer), _resident_slab(wu, layer),
                 _resident_slab(wd, layer), _resident((1, D_MODEL))]
    if mix is None:
        operands, mix_specs = (x,), [tile(D_MODEL)]
    else:
        ya, yb, wo, j = mix
        operands = (x, ya, yb, wo)
        mix_specs = [tile(D_MODEL), tile(ya.shape[1]), tile(yb.shape[1]), _resident_slab(wo, (j,))]
    return pl.pallas_call(
        functools.partial(_ffn_kernel, final_norm=final_norm, mixed=mix is not None),
        grid=(n // FFN_TILE,),
        in_specs=mix_specs + ffn_specs,
        out_specs=tile(D_MODEL),
        out_shape=jax.ShapeDtypeStruct(x.shape, F32),
        scratch_shapes=[pltpu.VMEM((FFN_TILE, D_MODEL), BF16)],
        compiler_params=_params("parallel"),
        name="ffn_mix" if mix is not None else "ffn",
    )(*operands, nw, wg, wu, wd, fw)


def _swap_pairs(x):
    lane = lax.broadcasted_iota(jnp.int32, x.shape, 1)
    return jnp.where(lane % 2 == 0, pltpu.roll(x, LANES - 1, 1), pltpu.roll(x, 1, 1))


def _head_norm_rope(xg, gain, cos, sin_signed, seg_ones):
    sq = xg * xg
    hi = sq.astype(BF16)
    lo = (sq - hi.astype(F32)).astype(BF16)
    ssum = jnp.dot(jnp.concatenate([hi, lo], axis=1), seg_ones, preferred_element_type=F32)
    xn = xg * lax.rsqrt(ssum * (1.0 / HEAD_DIM) + EPS) * gain
    return xn * cos + _swap_pairs(xn) * sin_signed


def _dup_halves(x):
    lane = lax.broadcasted_iota(jnp.int32, x.shape, 1)
    rolled = pltpu.roll(x, HEAD_DIM, 1)
    low = lane < HEAD_DIM
    return jnp.where(low, x, rolled), jnp.where(low, rolled, x)


def _ab_in_kernel(x_ref, nw_ref, w_ref, qg_ref, kg_ref, cos_ref, sin_ref, seg_ref,
                  aq_ref, ak_ref, av_ref, bq_ref, bk_ref, bv_ref, h_ref):
    h_ref[...] = _rms_rows(x_ref[...], nw_ref[...]).astype(BF16)
    h = h_ref[...]
    cos = cos_ref[...]
    sin = sin_ref[...]
    seg = seg_ref[...]

    def proj(lo, width):
        return jnp.dot(h, w_ref[:, lo:lo + width], preferred_element_type=F32)

    aq = proj(0, A_Q)
    for j in range(A_Q // LANES):
        q = _head_norm_rope(aq[:, j * LANES:(j + 1) * LANES], qg_ref[...], cos, sin, seg)
        aq_ref[:, j * LANES:(j + 1) * LANES] = (q * Q_SCALE).astype(BF16)
    akv = proj(A_Q, 2 * A_KV)
    k0, k1 = _dup_halves(_head_norm_rope(akv[:, :A_KV], kg_ref[...], cos, sin, seg))
    ak_ref[:, :LANES] = k0.astype(BF16)
    ak_ref[:, LANES:] = k1.astype(BF16)
    v0, v1 = _dup_halves(akv[:, A_KV:])
    av_ref[:, :LANES] = v0.astype(BF16)
    av_ref[:, LANES:] = v1.astype(BF16)
    base = A_Q + 2 * A_KV
    bq_ref[...] = (proj(base, B_W) * Q_SCALE).astype(BF16)
    bk_ref[...] = proj(base + B_W, B_W).astype(BF16)
    bv_ref[...] = proj(base + 2 * B_W, B_W).astype(BF16)


def _ab_in(x, nw, w_in, qg, kg, cos_t, sin_t, seg, seq):
    n = x.shape[0]
    ab_in = w_in.shape[1]
    tiles_per_seq = seq // TOKEN_TILE
    tile = lambda w: pl.BlockSpec((TOKEN_TILE, w), lambda i: (i, 0))
    rope = pl.BlockSpec((TOKEN_TILE, LANES), lambda i: (i % tiles_per_seq, 0))
    widths = (A_Q, 2 * LANES, 2 * LANES, B_W, B_W, B_W)
    return pl.pallas_call(
        _ab_in_kernel,
        grid=(n // TOKEN_TILE,),
        in_specs=[tile(D_MODEL), _resident((1, D_MODEL)), _resident((D_MODEL, ab_in)),
                  _resident((1, LANES)), _resident((1, LANES)), rope, rope,
                  _resident((2 * LANES, LANES))],
        out_specs=[tile(w) for w in widths],
        out_shape=[jax.ShapeDtypeStruct((n, w), BF16) for w in widths],
        scratch_shapes=[pltpu.VMEM((TOKEN_TILE, D_MODEL), BF16)],
        compiler_params=_params("parallel"),
        name="ab_in",
    )(x, nw, w_in, qg, kg, cos_t, sin_t, seg)


def _stack_heads(qp):
    lane = lax.broadcasted_iota(jnp.int32, qp.shape, 1)
    zero = jnp.zeros_like(qp)
    return jnp.concatenate([jnp.where(lane < HEAD_DIM, qp, zero),
                            jnp.where(lane < HEAD_DIM, zero, qp)], axis=0)


def _unstack_heads(o, rows):
    lane = lax.broadcasted_iota(jnp.int32, (rows, LANES), 1)
    return jnp.where(lane < HEAD_DIM, o[:rows], o[rows:])


def _softmax_pv(s, v_ones):
    m = jnp.max(s, axis=-1, keepdims=True)
    p = jnp.exp2(s - m).astype(BF16)
    pv = jnp.dot(p, v_ones, preferred_element_type=F32)
    return pv / pltpu.roll(pv, HEAD_DIM, 1)


_NT = (((1,), (1,)), ((), ()))


def _gqa_kernel(q_ref, k_ref, v_ref, o_ref):
    k = k_ref[0]
    v = v_ref[0]
    vlane = lax.broadcasted_iota(jnp.int32, v.shape, 1)
    one = jnp.ones_like(v)
    v_ones = (jnp.where(vlane < HEAD_DIM, v, one), jnp.where(vlane < HEAD_DIM, one, v))
    lane = lax.broadcasted_iota(jnp.int32, (Q_CHAIN, LANES), 1)
    masks = (lane < HEAD_DIM, lane >= HEAD_DIM)
    chains = [(blk, h) for blk in range(Q_TILE // Q_CHAIN) for h in range(A_HEADS // A_KV_HEADS)]

    def block(c):
        blk, h = chains[c]
        return slice(blk * Q_CHAIN, (blk + 1) * Q_CHAIN), slice((h // 2) * LANES, (h // 2 + 1) * LANES)

    def scores(c):
        rows, cols = block(c)
        qp = q_ref[0, rows, cols]
        qh = jnp.where(masks[chains[c][1] % 2], qp, jnp.zeros_like(qp))
        return lax.dot_general(qh, k, _NT, preferred_element_type=F32)

    s_next = scores(0)
    outs = []
    for c in range(len(chains)):
        s_cur = s_next
        if c + 1 < len(chains):
            s_next = scores(c + 1)
        outs.append(_softmax_pv(s_cur, v_ones[chains[c][1] % 2]))
        if c % 2 == 1:
            rows, cols = block(c)
            o_ref[0, rows, cols] = jnp.where(lane < HEAD_DIM, outs[c - 1], outs[c]).astype(BF16)


def _gqa(aq, akd, avd):
    b, s, _ = aq.shape
    qspec = pl.BlockSpec((1, Q_TILE, 2 * LANES), lambda bi, g, qi: (bi, qi, g))
    kvspec = pl.BlockSpec((1, s, LANES), lambda bi, g, qi: (bi, 0, g))
    return pl.pallas_call(
        _gqa_kernel,
        grid=(b, A_KV_HEADS, s // Q_TILE),
        in_specs=[qspec, kvspec, kvspec],
        out_specs=qspec,
        out_shape=jax.ShapeDtypeStruct(aq.shape, BF16),
        compiler_params=_params("parallel", "parallel", "arbitrary"),
        name="gqa",
    )(aq, akd, avd)


def _na_kernel(q_ref, k_ref, v_ref, bias_ref, o_ref, *, rows):
    band = NA_ROWS * GRID_W
    pairs = B_W // LANES
    contract_rows = (((0,), (0,)), ((), ()))

    def row_group(i, carry):
        chains = []
        for rr in range(NA_ROWS_PER_STEP):
            r = i * NA_ROWS_PER_STEP + rr
            rs = jnp.clip(r - NA_ROWS // 2, 0, rows - NA_ROWS)
            q0 = pl.multiple_of(r * GRID_W, GRID_W)
            k0 = pl.multiple_of(rs * GRID_W, GRID_W)
            chains += [(r - rs, q0, k0, j) for j in range(pairs)]

        def scores(c):
            d, q0, k0, j = chains[c]
            cols = slice(j * LANES, (j + 1) * LANES)
            qs = _stack_heads(q_ref[0, pl.ds(q0, GRID_W), cols])
            kb = k_ref[0, pl.ds(k0, band), cols]
            return lax.dot_general(kb, qs, _NT, preferred_element_type=F32) + bias_ref[d, j]

        s_next = scores(0)
        for c in range(len(chains)):
            s_cur = s_next
            if c + 1 < len(chains):
                s_next = scores(c + 1)
            d, q0, k0, j = chains[c]
            cols = slice(j * LANES, (j + 1) * LANES)
            m = jnp.max(s_cur, axis=0, keepdims=True)
            p = jnp.exp2(s_cur - m)
            l = jnp.sum(p, axis=0, keepdims=True)
            pn = (p * (1.0 / l)).astype(BF16)
            o = lax.dot_general(pn, v_ref[0, pl.ds(k0, band), cols], contract_rows, preferred_element_type=F32)
            o_ref[0, pl.ds(q0, GRID_W), cols] = _unstack_heads(o, GRID_W).astype(BF16)
        return carry

    lax.fori_loop(0, rows // NA_ROWS_PER_STEP, row_group, 0)


def _na(bq, bk, bv, bias):
    b, s, w = bq.shape
    rows = s // GRID_W
    spec = pl.BlockSpec((1, s, w), lambda bi: (bi, 0, 0))
    return pl.pallas_call(
        functools.partial(_na_kernel, rows=rows),
        grid=(b,),
        in_specs=[spec, spec, spec, _resident(bias.shape)],
        out_specs=spec,
        out_shape=jax.ShapeDtypeStruct(bq.shape, BF16),
        compiler_params=_params("parallel"),
        name="na",
    )(bq, bk, bv, bias)


def _na_bias(rpb, rows):
    wr = min(NA_ROWS, rows)
    qc = np.arange(GRID_W)[:, None]
    kc = np.arange(GRID_W)[None, :]
    col_off = np.clip(kc - qc, -(NA_COLS - 1), NA_COLS - 1) + NA_COLS - 1
    wcs = np.clip(qc - NA_COLS // 2, 0, GRID_W - NA_COLS)
    valid = (kc >= wcs) & (kc < wcs + NA_COLS)
    onehot = (col_off[None] == np.arange(2 * NA_COLS - 1)[:, None, None]).astype(np.float32)
    by_col = jnp.einsum("hrc,cqk->hrqk", rpb.astype(F32), jnp.asarray(onehot),
                        precision=lax.Precision.HIGHEST)
    t = jnp.stack([by_col[:, NA_ROWS - 1 - d:NA_ROWS - 1 - d + wr] for d in range(wr)])
    t = jnp.where(valid[None, None, None], t * float(np.log2(np.e)), NEG_INF)
    t = t.reshape(wr, B_HEADS // 2, 2, wr, GRID_W, GRID_W)
    t = t.transpose(0, 1, 3, 5, 2, 4)
    return t.reshape(wr, B_HEADS // 2, wr * GRID_W, 2 * GRID_W)


def _cd_in_kernel(x_ref, nw_ref, w_ref, zc_ref, cb_ref, zd_ref, h_ref):
    h_ref[...] = _rms_rows(x_ref[...], nw_ref[...]).astype(BF16)
    h = h_ref[...]

    def proj(i):
        return jnp.dot(h, w_ref[:, i * C_WIDTH:(i + 1) * C_WIDTH], preferred_element_type=F32)

    zc_ref[...] = (proj(2) * proj(0)).astype(BF16)
    cb_ref[...] = proj(1).astype(BF16)
    zd_ref[...] = (proj(3) * jax.nn.sigmoid(proj(4))).astype(BF16)


def _cd_in(x, nw, w_in):
    n = x.shape[0]
    tile = lambda w: pl.BlockSpec((TOKEN_TILE, w), lambda i: (i, 0))
    return pl.pallas_call(
        _cd_in_kernel,
        grid=(n // TOKEN_TILE,),
        in_specs=[tile(D_MODEL), _resident((1, D_MODEL)), _resident(w_in.shape)],
        out_specs=[tile(C_WIDTH)] * 3,
        out_shape=[jax.ShapeDtypeStruct((n, C_WIDTH), BF16)] * 3,
        scratch_shapes=[pltpu.VMEM((TOKEN_TILE, D_MODEL), BF16)],
        compiler_params=_params("parallel"),
        name="cd_in",
    )(x, nw, w_in)


def _conv_kernel(zc_ref, cb_ref, zd_ref, wc_ref, wd_ref, g_ref, b_ref, yc_ref, yd_ref,
                 pad_ref, raw_ref, *, seq):
    halo = jnp.zeros((D_HALO, C_WIDTH), F32)
    nchunks = seq // CONV_ROWS
    window = CONV_ROWS + 2 * D_HALO

    def conv(src_ref, w_ref, taps, emit):
        pad_ref[:D_HALO] = halo
        pad_ref[D_HALO + seq:] = halo
        pad_ref[D_HALO:D_HALO + seq] = src_ref[0].astype(F32)
        first = D_HALO - taps // 2
        for c in range(C_WIDTH // LANES):
            cols = slice(c * LANES, (c + 1) * LANES)
            w = w_ref[:, cols]

            def chunk(i, carry):
                r0 = pl.multiple_of(i * CONV_ROWS, CONV_ROWS)
                win = pad_ref[pl.ds(r0, window), cols]
                acc = jnp.zeros((CONV_ROWS, LANES), F32)
                for sh in range(SUBLANES):
                    offs = [first + k for k in range(taps) if (first + k) % SUBLANES == sh]
                    if not offs:
                        continue
                    shifted = win if sh == 0 else pltpu.roll(win, window - sh, 0)
                    for off in offs:
                        k = off - first
                        base = off - sh
                        acc = acc + w[k:k + 1, :] * shifted[base:base + CONV_ROWS]
                emit(r0, cols, acc)
                return carry

            lax.fori_loop(0, nchunks, chunk, 0, unroll=4)

    def emit_c(r0, cols, acc):
        yc_ref[0, pl.ds(r0, CONV_ROWS), cols] = (
            cb_ref[0, pl.ds(r0, CONV_ROWS), cols].astype(F32) * acc).astype(BF16)

    def emit_d(r0, cols, acc):
        raw_ref[pl.ds(r0, CONV_ROWS), cols] = acc

    conv(zc_ref, wc_ref, C_CONV, emit_c)
    conv(zd_ref, wd_ref, D_CONV, emit_d)

    def norm(i, carry):
        r0 = pl.multiple_of(i * CONV_ROWS, CONV_ROWS)
        y = raw_ref[pl.ds(r0, CONV_ROWS), :]
        mu = jnp.mean(y, axis=-1, keepdims=True)
        var = jnp.mean(jnp.square(y - mu), axis=-1, keepdims=True)
        z = (y - mu) * lax.rsqrt(var + EPS) * g_ref[...] + b_ref[...]
        yd_ref[0, pl.ds(r0, CONV_ROWS), :] = (z * jax.nn.sigmoid(z)).astype(BF16)
        return carry

    lax.fori_loop(0, nchunks, norm, 0, unroll=4)


def _conv(zc, cb, zd, wc, wd, g, b):
    bsz, seq, w = zc.shape
    padded = seq + 2 * D_HALO
    spec = pl.BlockSpec((1, seq, w), lambda bi: (bi, 0, 0))
    return pl.pallas_call(
        functools.partial(_conv_kernel, seq=seq),
        grid=(bsz,),
        in_specs=[spec, spec, spec, _resident(wc.shape), _resident(wd.shape),
                  _resident((1, w)), _resident((1, w))],
        out_specs=[spec, spec],
        out_shape=[jax.ShapeDtypeStruct(zc.shape, BF16)] * 2,
        scratch_shapes=[pltpu.VMEM((padded, w), F32),
                        pltpu.VMEM((seq, w), F32)],
        compiler_params=_params("parallel"),
        name="conv",
    )(zc, cb, zd, wc, wd, g, b)


def _rope_tables(seq):
    t = jnp.arange(seq)
    row = (t // GRID_W).astype(F32)
    col = (t % GRID_W).astype(F32)
    half = HEAD_DIM // 2
    freqs = ROPE_THETA ** (-jnp.arange(0, half, 2, dtype=F32) / half)
    ang = jnp.concatenate([row[:, None] * freqs, col[:, None] * freqs], axis=-1)
    cos = jnp.repeat(jnp.cos(ang), 2, axis=-1)
    sin = jnp.repeat(jnp.sin(ang), 2, axis=-1)
    sign = jnp.tile(jnp.array([-1.0, 1.0], F32), half)
    return jnp.tile(cos, (1, 2)), jnp.tile(sin * sign, (1, 2))


def kernel(x, ffn_norm, mix_norm, ffn_w_gate, ffn_w_up, ffn_w_down, ab_w_in, ab_w_out,
           a_q_norm, a_k_norm, b_rpb, cd_w_in, cd_w_out, c_conv_w, d_conv_w,
           d_norm_g, d_norm_b, final_norm):
    bsz, seq, d = x.shape
    assert d == D_MODEL and seq % TOKEN_TILE == 0 and seq % GRID_W == 0
    n = bsz * seq
    rows = seq // GRID_W
    cos_t, sin_t = _rope_tables(seq)
    seg = jnp.asarray(np.tile(np.kron(np.eye(LANES // HEAD_DIM), np.ones((HEAD_DIM, HEAD_DIM))), (2, 1)), BF16)
    row_vec = lambda v: v.reshape(1, -1).astype(F32)

    wg, wu, wd = (w.astype(BF16) for w in (ffn_w_gate, ffn_w_up, ffn_w_down))
    ab_wo, cd_wo = ab_w_out.astype(BF16), cd_w_out.astype(BF16)

    def ffn(xf, i, k, mix=None):
        last = i == DEPTH - 1 and k == 1
        return _ffn(xf, row_vec(ffn_norm[i, k]), wg, wu, wd, (i, k),
                    row_vec(final_norm), final_norm=last, mix=mix)

    xf = x.reshape(n, d)
    for i in range(DEPTH):
        xf = ffn(xf, i, 0)
        j = i // 2
        if i % 2 == 0:
            two = lambda g: row_vec(jnp.tile(g, LANES // HEAD_DIM))
            aq, akd, avd, bq, bk, bv = _ab_in(
                xf, row_vec(mix_norm[i]), ab_w_in[j].astype(BF16), two(a_q_norm[j]),
                two(a_k_norm[j]), cos_t, sin_t, seg, seq)
            r3 = lambda a: a.reshape(bsz, seq, a.shape[1])
            ya = _gqa(r3(aq), r3(akd), r3(avd))
            yb = _na(r3(bq), r3(bk), r3(bv), _na_bias(b_rpb[j], rows))
            mix = (ya.reshape(n, A_Q), yb.reshape(n, B_W), ab_wo, j)
        else:
            zc, cb, zd = _cd_in(xf, row_vec(mix_norm[i]), cd_w_in[j].astype(BF16))
            r3 = lambda a: a.reshape(bsz, seq, a.shape[1])
            yc, yd = _conv(r3(zc), r3(cb), r3(zd), c_conv_w[j].astype(F32), d_conv_w[j].astype(F32),
                           row_vec(d_norm_g[j]), row_vec(d_norm_b[j]))
            mix = (yc.reshape(n, C_WIDTH), yd.reshape(n, D_WIDTH), cd_wo, j)
        xf = ffn(xf, i, 1, mix)
    return xf.reshape(bsz, seq, d)
```

```python
import functools

import jax
import jax.numpy as jnp
import numpy as np
from jax import lax
from jax.experimental import pallas as pl
from jax.experimental.pallas import tpu as pltpu

D_MODEL = 1024
DEPTH = 4
HEAD_DIM = 64
A_HEADS = 8
A_KV_HEADS = 2
B_HEADS = 8
C_WIDTH = 512
D_WIDTH = 512
C_CONV = 3
D_CONV = 31
D_FF = 2816
GRID_W = 64
NA_ROWS = 8
NA_COLS = 16
ROPE_THETA = 10000.0
EPS = 1e-6
NEG_INF = -1e30
A_Q = A_HEADS * HEAD_DIM
A_KV = A_KV_HEADS * HEAD_DIM
B_W = B_HEADS * HEAD_DIM

LANES = 128
SUBLANES = 8
VMEM_LIMIT_BYTES = 56 * 1024 * 1024
TOKEN_TILE = 512
FFN_TILE = 1024
FFN_SUB_TILE = 512
FF_CHUNK = 256
Q_TILE = 1024
Q_CHAIN = 512
NA_ROWS_PER_STEP = 8
CONV_ROWS = 128
D_HALO = 16

F32 = jnp.float32
BF16 = jnp.bfloat16
Q_SCALE = HEAD_DIM ** -0.5 * float(np.log2(np.e))


def _params(*sem):
    return pltpu.CompilerParams(dimension_semantics=sem, vmem_limit_bytes=VMEM_LIMIT_BYTES)


def _resident(shape):
    zeros = (0,) * len(shape)
    return pl.BlockSpec(shape, lambda *_: zeros, pipeline_mode=pl.Buffered(1))


def _resident_slab(stacked, index):
    tail = stacked.shape[len(index):]
    at = tuple(index) + (0,) * len(tail)
    return pl.BlockSpec((None,) * len(index) + tail, lambda *_: at, pipeline_mode=pl.Buffered(1))


def _rms_rows(x, gain):
    ms = jnp.mean(x * x, axis=-1, keepdims=True)
    return x * lax.rsqrt(ms + EPS) * gain


def _ffn_kernel(*refs, final_norm, mixed):
    if mixed:
        x_ref, ya_ref, yb_ref, wo_ref, nw_ref, wg_ref, wu_ref, wd_ref, fw_ref, o_ref, h_ref = refs
        half = ya_ref.shape[1]
    else:
        x_ref, nw_ref, wg_ref, wu_ref, wd_ref, fw_ref, o_ref, h_ref = refs
    nch = D_FF // FF_CHUNK
    subs = [slice(t * FFN_SUB_TILE, (t + 1) * FFN_SUB_TILE) for t in range(FFN_TILE // FFN_SUB_TILE)]
    for rows in subs:
        x1 = x_ref[rows, :]
        if mixed:
            x1 = (x1 + jnp.dot(ya_ref[rows, :], wo_ref[:half], preferred_element_type=F32)
                  + jnp.dot(yb_ref[rows, :], wo_ref[half:], preferred_element_type=F32))
        h_ref[rows, :] = _rms_rows(x1, nw_ref[...]).astype(BF16)
        o_ref[rows, :] = x1
    for rows in subs:
        for c in range(nch):
            cols = slice(c * FF_CHUNK, (c + 1) * FF_CHUNK)
            h = h_ref[rows, :]
            g = jnp.dot(h, wg_ref[:, cols], preferred_element_type=F32)
            u = jnp.dot(h, wu_ref[:, cols], preferred_element_type=F32)
            a = (g * jax.nn.sigmoid(g) * u).astype(BF16)
            o_ref[rows, :] += 0.5 * jnp.dot(a, wd_ref[cols, :], preferred_element_type=F32)
        if final_norm:
            o_ref[rows, :] = _rms_rows(o_ref[rows, :], fw_ref[...])


def _ffn(x, nw, wg, wu, wd, layer, fw, final_norm, mix=None):
    n = x.shape[0]
    tile = lambda w: pl.BlockSpec((FFN_TILE, w), lambda i: (i, 0))
    ffn_specs = [_resident((1, D_MODEL)), _resident_slab(wg, layer), _resident_slab(wu, layer),
                 _resident_slab(wd, layer), _resident((1, D_MODEL))]
    if mix is None:
        operands, mix_specs = (x,), [tile(D_MODEL)]
    else:
        ya, yb, wo, j = mix
        operands = (x, ya, yb, wo)
        mix_specs = [tile(D_MODEL), tile(ya.shape[1]), tile(yb.shape[1]), _resident_slab(wo, (j,))]
    return pl.pallas_call(
        functools.partial(_ffn_kernel, final_norm=final_norm, mixed=mix is not None),
        grid=(n // FFN_TILE,),
        in_specs=mix_specs + ffn_specs,
        out_specs=tile(D_MODEL),
        out_shape=jax.ShapeDtypeStruct(x.shape, F32),
        scratch_shapes=[pltpu.VMEM((FFN_TILE, D_MODEL), BF16)],
        compiler_params=_params("parallel"),
        name="ffn_mix" if mix is not None else "ffn",
    )(*operands, nw, wg, wu, wd, fw)


def _swap_pairs(x):
    lane = lax.broadcasted_iota(jnp.int32, x.shape, 1)
    return jnp.where(lane % 2 == 0, pltpu.roll(x, LANES - 1, 1), pltpu.roll(x, 1, 1))


def _head_norm_rope(xg, gain, cos, sin_signed, seg_ones):
    sq = xg * xg
    hi = sq.astype(BF16)
    lo = (sq - hi.astype(F32)).astype(BF16)
    ssum = jnp.dot(jnp.concatenate([hi, lo], axis=1), seg_ones, preferred_element_type=F32)
    xn = xg * lax.rsqrt(ssum * (1.0 / HEAD_DIM) + EPS) * gain
    return xn * cos + _swap_pairs(xn) * sin_signed


def _dup_halves(x):
    lane = lax.broadcasted_iota(jnp.int32, x.shape, 1)
    rolled = pltpu.roll(x, HEAD_DIM, 1)
    low = lane < HEAD_DIM
    return jnp.where(low, x, rolled), jnp.where(low, rolled, x)


def _ab_in_kernel(x_ref, nw_ref, w_ref, qg_ref, kg_ref, cos_ref, sin_ref, seg_ref,
                  aq_ref, ak_ref, av_ref, bq_ref, bk_ref, bv_ref, h_ref):
    h_ref[...] = _rms_rows(x_ref[...], nw_ref[...]).astype(BF16)
    h = h_ref[...]
    cos = cos_ref[...]
    sin = sin_ref[...]
    seg = seg_ref[...]

    def proj(lo, width):
        return jnp.dot(h, w_ref[:, lo:lo + width], preferred_element_type=F32)

    aq = proj(0, A_Q)
    for j in range(A_Q // LANES):
        q = _head_norm_rope(aq[:, j * LANES:(j + 1) * LANES], qg_ref[...], cos, sin, seg)
        aq_ref[:, j * LANES:(j + 1) * LANES] = (q * Q_SCALE).astype(BF16)
    akv = proj(A_Q, 2 * A_KV)
    k0, k1 = _dup_halves(_head_norm_rope(akv[:, :A_KV], kg_ref[...], cos, sin, seg))
    ak_ref[:, :LANES] = k0.astype(BF16)
    ak_ref[:, LANES:] = k1.astype(BF16)
    v0, v1 = _dup_halves(akv[:, A_KV:])
    av_ref[:, :LANES] = v0.astype(BF16)
    av_ref[:, LANES:] = v1.astype(BF16)
    base = A_Q + 2 * A_KV
    bq_ref[...] = (proj(base, B_W) * Q_SCALE).astype(BF16)
    bk_ref[...] = proj(base + B_W, B_W).astype(BF16)
    bv_ref[...] = proj(base + 2 * B_W, B_W).astype(BF16)


def _ab_in(x, nw, w_in, qg, kg, cos_t, sin_t, seg, seq):
    n = x.shape[0]
    ab_in = w_in.shape[1]
    tiles_per_seq = seq // TOKEN_TILE
    tile = lambda w: pl.BlockSpec((TOKEN_TILE, w), lambda i: (i, 0))
    rope = pl.BlockSpec((TOKEN_TILE, LANES), lambda i: (i % tiles_per_seq, 0))
    widths = (A_Q, 2 * LANES, 2 * LANES, B_W, B_W, B_W)
    return pl.pallas_call(
        _ab_in_kernel,
        grid=(n // TOKEN_TILE,),
        in_specs=[tile(D_MODEL), _resident((1, D_MODEL)), _resident((D_MODEL, ab_in)),
                  _resident((1, LANES)), _resident((1, LANES)), rope, rope,
                  _resident((2 * LANES, LANES))],
        out_specs=[tile(w) for w in widths],
        out_shape=[jax.ShapeDtypeStruct((n, w), BF16) for w in widths],
        scratch_shapes=[pltpu.VMEM((TOKEN_TILE, D_MODEL), BF16)],
        compiler_params=_params("parallel"),
        name="ab_in",
    )(x, nw, w_in, qg, kg, cos_t, sin_t, seg)


def _stack_heads(qp):
    lane = lax.broadcasted_iota(jnp.int32, qp.shape, 1)
    zero = jnp.zeros_like(qp)
    return jnp.concatenate([jnp.where(lane < HEAD_DIM, qp, zero),
                            jnp.where(lane < HEAD_DIM, zero, qp)], axis=0)


def _unstack_heads(o, rows):
    lane = lax.broadcasted_iota(jnp.int32, (rows, LANES), 1)
    return jnp.where(lane < HEAD_DIM, o[:rows], o[rows:])


def _softmax_pv(s, v_ones):
    m = jnp.max(s, axis=-1, keepdims=True)
    p = jnp.exp2(s - m).astype(BF16)
    pv = jnp.dot(p, v_ones, preferred_element_type=F32)
    return pv / pltpu.roll(pv, HEAD_DIM, 1)


_NT = (((1,), (1,)), ((), ()))


def _gqa_kernel(q_ref, k_ref, v_ref, o_ref):
    k = k_ref[0]
    v = v_ref[0]
    vlane = lax.broadcasted_iota(jnp.int32, v.shape, 1)
    one = jnp.ones_like(v)
    v_ones = (jnp.where(vlane < HEAD_DIM, v, one), jnp.where(vlane < HEAD_DIM, one, v))
    lane = lax.broadcasted_iota(jnp.int32, (Q_CHAIN, LANES), 1)
    masks = (lane < HEAD_DIM, lane >= HEAD_DIM)
    chains = [(blk, h) for blk in range(Q_TILE // Q_CHAIN) for h in range(A_HEADS // A_KV_HEADS)]

    def block(c):
        blk, h = chains[c]
        return slice(blk * Q_CHAIN, (blk + 1) * Q_CHAIN), slice((h // 2) * LANES, (h // 2 + 1) * LANES)

    def scores(c):
        rows, cols = block(c)
        qp = q_ref[0, rows, cols]
        qh = jnp.where(masks[chains[c][1] % 2], qp, jnp.zeros_like(qp))
        return lax.dot_general(qh, k, _NT, preferred_element_type=F32)

    s_next = scores(0)
    outs = []
    for c in range(len(chains)):
        s_cur = s_next
        if c + 1 < len(chains):
            s_next = scores(c + 1)
        outs.append(_softmax_pv(s_cur, v_ones[chains[c][1] % 2]))
        if c % 2 == 1:
            rows, cols = block(c)
            o_ref[0, rows, cols] = jnp.where(lane < HEAD_DIM, outs[c - 1], outs[c]).astype(BF16)


def _gqa(aq, akd, avd):
    b, s, _ = aq.shape
    qspec = pl.BlockSpec((1, Q_TILE, 2 * LANES), lambda bi, g, qi: (bi, qi, g))
    kvspec = pl.BlockSpec((1, s, LANES), lambda bi, g, qi: (bi, 0, g))
    return pl.pallas_call(
        _gqa_kernel,
        grid=(b, A_KV_HEADS, s // Q_TILE),
        in_specs=[qspec, kvspec, kvspec],
        out_specs=qspec,
        out_shape=jax.ShapeDtypeStruct(aq.shape, BF16),
        compiler_params=_params("parallel", "parallel", "arbitrary"),
        name="gqa",
    )(aq, akd, avd)


def _na_kernel(q_ref, k_ref, v_ref, bias_ref, o_ref, *, rows):
    band = NA_ROWS * GRID_W
    pairs = B_W // LANES
    contract_rows = (((0,), (0,)), ((), ()))

    def row_group(i, carry):
        chains = []
        for rr in range(NA_ROWS_PER_STEP):
            r = i * NA_ROWS_PER_STEP + rr
            rs = jnp.clip(r - NA_ROWS // 2, 0, rows - NA_ROWS)
            q0 = pl.multiple_of(r * GRID_W, GRID_W)
            k0 = pl.multiple_of(rs * GRID_W, GRID_W)
            chains += [(r - rs, q0, k0, j) for j in range(pairs)]

        def scores(c):
            d, q0, k0, j = chains[c]
            cols = slice(j * LANES, (j + 1) * LANES)
            qs = _stack_heads(q_ref[0, pl.ds(q0, GRID_W), cols])
            kb = k_ref[0, pl.ds(k0, band), cols]
            return lax.dot_general(kb, qs, _NT, preferred_element_type=F32) + bias_ref[d, j]

        s_next = scores(0)
        for c in range(len(chains)):
            s_cur = s_next
            if c + 1 < len(chains):
                s_next = scores(c + 1)
            d, q0, k0, j = chains[c]
            cols = slice(j * LANES, (j + 1) * LANES)
            m = jnp.max(s_cur, axis=0, keepdims=True)
            p = jnp.exp2(s_cur - m)
            l = jnp.sum(p, axis=0, keepdims=True)
            pn = (p * (1.0 / l)).astype(BF16)
            o = lax.dot_general(pn, v_ref[0, pl.ds(k0, band), cols], contract_rows, preferred_element_type=F32)
            o_ref[0, pl.ds(q0, GRID_W), cols] = _unstack_heads(o, GRID_W).astype(BF16)
        return carry

    lax.fori_loop(0, rows // NA_ROWS_PER_STEP, row_group, 0)


def _na(bq, bk, bv, bias):
    b, s, w = bq.shape
    rows = s // GRID_W
    spec = pl.BlockSpec((1, s, w), lambda bi: (bi, 0, 0))
    return pl.pallas_call(
        functools.partial(_na_kernel, rows=rows),
        grid=(b,),
        in_specs=[spec, spec, spec, _resident(bias.shape)],
        out_specs=spec,
        out_shape=jax.ShapeDtypeStruct(bq.shape, BF16),
        compiler_params=_params("parallel"),
        name="na",
    )(bq, bk, bv, bias)


def _na_bias(rpb, rows):
    wr = min(NA_ROWS, rows)
    qc = np.arange(GRID_W)[:, None]
    kc = np.arange(GRID_W)[None, :]
    col_off = np.clip(kc - qc, -(NA_COLS - 1), NA_COLS - 1) + NA_COLS - 1
    wcs = np.clip(qc - NA_COLS // 2, 0, GRID_W - NA_COLS)
    valid = (kc >= wcs) & (kc < wcs + NA_COLS)
    onehot = (col_off[None] == np.arange(2 * NA_COLS - 1)[:, None, None]).astype(np.float32)
    by_col = jnp.einsum("hrc,cqk->hrqk", rpb.astype(F32), jnp.asarray(onehot),
                        precision=lax.Precision.HIGHEST)
    by_col = by_col.reshape(B_HEADS // 2, 2, 2 * NA_ROWS - 1, GRID_W, GRID_W).transpose(0, 2, 4, 1, 3)
    by_col = by_col * float(np.log2(np.e))
    t = jnp.stack([by_col[:, NA_ROWS - 1 - d:NA_ROWS - 1 - d + wr] for d in range(wr)])
    t = jnp.where(valid.T[None, None, None, :, None, :], t, NEG_INF)
    return t.reshape(wr, B_HEADS // 2, wr * GRID_W, 2 * GRID_W)


def _cd_in_kernel(x_ref, nw_ref, w_ref, zc_ref, cb_ref, zd_ref, h_ref):
    h_ref[...] = _rms_rows(x_ref[...], nw_ref[...]).astype(BF16)
    h = h_ref[...]

    def proj(i):
        return jnp.dot(h, w_ref[:, i * C_WIDTH:(i + 1) * C_WIDTH], preferred_element_type=F32)

    zc_ref[...] = (proj(2) * proj(0)).astype(BF16)
    cb_ref[...] = proj(1).astype(BF16)
    zd_ref[...] = (proj(3) * jax.nn.sigmoid(proj(4))).astype(BF16)


def _cd_in(x, nw, w_in):
    n = x.shape[0]
    tile = lambda w: pl.BlockSpec((TOKEN_TILE, w), lambda i: (i, 0))
    return pl.pallas_call(
        _cd_in_kernel,
        grid=(n // TOKEN_TILE,),
        in_specs=[tile(D_MODEL), _resident((1, D_MODEL)), _resident(w_in.shape)],
        out_specs=[tile(C_WIDTH)] * 3,
        out_shape=[jax.ShapeDtypeStruct((n, C_WIDTH), BF16)] * 3,
        scratch_shapes=[pltpu.VMEM((TOKEN_TILE, D_MODEL), BF16)],
        compiler_params=_params("parallel"),
        name="cd_in",
    )(x, nw, w_in)


def _conv_kernel(zc_ref, cb_ref, zd_ref, wc_ref, wd_ref, g_ref, b_ref, yc_ref, yd_ref,
                 pad_ref, raw_ref, *, seq):
    halo = jnp.zeros((D_HALO, C_WIDTH), F32)
    nchunks = seq // CONV_ROWS
    window = CONV_ROWS + 2 * D_HALO

    def conv(src_ref, w_ref, taps, emit):
        pad_ref[:D_HALO] = halo
        pad_ref[D_HALO + seq:] = halo
        pad_ref[D_HALO:D_HALO + seq] = src_ref[0].astype(F32)
        first = D_HALO - taps // 2
        for c in range(C_WIDTH // LANES):
            cols = slice(c * LANES, (c + 1) * LANES)
            w = w_ref[:, cols]

            def chunk(i, carry):
                r0 = pl.multiple_of(i * CONV_ROWS, CONV_ROWS)
                win = pad_ref[pl.ds(r0, window), cols]
                acc = jnp.zeros((CONV_ROWS, LANES), F32)
                for sh in range(SUBLANES):
                    offs = [first + k for k in range(taps) if (first + k) % SUBLANES == sh]
                    if not offs:
                        continue
                    shifted = win if sh == 0 else pltpu.roll(win, window - sh, 0)
                    for off in offs:
                        k = off - first
                        base = off - sh
                        acc = acc + w[k:k + 1, :] * shifted[base:base + CONV_ROWS]
                emit(r0, cols, acc)
                return carry

            lax.fori_loop(0, nchunks, chunk, 0, unroll=4)

    def emit_c(r0, cols, acc):
        yc_ref[0, pl.ds(r0, CONV_ROWS), cols] = (
            cb_ref[0, pl.ds(r0, CONV_ROWS), cols].astype(F32) * acc).astype(BF16)

    def emit_d(r0, cols, acc):
        raw_ref[pl.ds(r0, CONV_ROWS), cols] = acc

    conv(zc_ref, wc_ref, C_CONV, emit_c)
    conv(zd_ref, wd_ref, D_CONV, emit_d)

    def norm(i, carry):
        r0 = pl.multiple_of(i * CONV_ROWS, CONV_ROWS)
        y = raw_ref[pl.ds(r0, CONV_ROWS), :]
        mu = jnp.mean(y, axis=-1, keepdims=True)
        var = jnp.mean(jnp.square(y - mu), axis=-1, keepdims=True)
        z = (y - mu) * lax.rsqrt(var + EPS) * g_ref[...] + b_ref[...]
        yd_ref[0, pl.ds(r0, CONV_ROWS), :] = (z * jax.nn.sigmoid(z)).astype(BF16)
        return carry

    lax.fori_loop(0, nchunks, norm, 0, unroll=4)


def _conv(zc, cb, zd, wc, wd, g, b):
    bsz, seq, w = zc.shape
    padded = seq + 2 * D_HALO
    spec = pl.BlockSpec((1, seq, w), lambda bi: (bi, 0, 0))
    return pl.pallas_call(
        functools.partial(_conv_kernel, seq=seq),
        grid=(bsz,),
        in_specs=[spec, spec, spec, _resident(wc.shape), _resident(wd.shape),
                  _resident((1, w)), _resident((1, w))],
        out_specs=[spec, spec],
        out_shape=[jax.ShapeDtypeStruct(zc.shape, BF16)] * 2,
        scratch_shapes=[pltpu.VMEM((padded, w), F32),
                        pltpu.VMEM((seq, w), F32)],
        compiler_params=_params("parallel"),
        name="conv",
    )(zc, cb, zd, wc, wd, g, b)


def _rope_tables(seq):
    t = jnp.arange(seq)
    row = (t // GRID_W).astype(F32)
    col = (t % GRID_W).astype(F32)
    half = HEAD_DIM // 2
    freqs = ROPE_THETA ** (-jnp.arange(0, half, 2, dtype=F32) / half)
    ang = jnp.concatenate([row[:, None] * freqs, col[:, None] * freqs], axis=-1)
    cos = jnp.repeat(jnp.cos(ang), 2, axis=-1)
    sin = jnp.repeat(jnp.sin(ang), 2, axis=-1)
    sign = jnp.tile(jnp.array([-1.0, 1.0], F32), half)
    return jnp.tile(cos, (1, 2)), jnp.tile(sin * sign, (1, 2))


def kernel(x, ffn_norm, mix_norm, ffn_w_gate, ffn_w_up, ffn_w_down, ab_w_in, ab_w_out,
           a_q_norm, a_k_norm, b_rpb, cd_w_in, cd_w_out, c_conv_w, d_conv_w,
           d_norm_g, d_norm_b, final_norm):
    bsz, seq, d = x.shape
    assert d == D_MODEL and seq % TOKEN_TILE == 0 and seq % GRID_W == 0
    n = bsz * seq
    rows = seq // GRID_W
    cos_t, sin_t = _rope_tables(seq)
    seg = jnp.asarray(np.tile(np.kron(np.eye(LANES // HEAD_DIM), np.ones((HEAD_DIM, HEAD_DIM))), (2, 1)), BF16)
    row_vec = lambda v: v.reshape(1, -1).astype(F32)

    wg, wu, wd = (w.astype(BF16) for w in (ffn_w_gate, ffn_w_up, ffn_w_down))
    ab_wo, cd_wo = ab_w_out.astype(BF16), cd_w_out.astype(BF16)

    def ffn(xf, i, k, mix=None):
        last = i == DEPTH - 1 and k == 1
        return _ffn(xf, row_vec(ffn_norm[i, k]), wg, wu, wd, (i, k),
                    row_vec(final_norm), final_norm=last, mix=mix)

    xf = x.reshape(n, d)
    for i in range(DEPTH):
        xf = ffn(xf, i, 0)
        j = i // 2
        if i % 2 == 0:
            two = lambda g: row_vec(jnp.tile(g, LANES // HEAD_DIM))
            aq, akd, avd, bq, bk, bv = _ab_in(
                xf, row_vec(mix_norm[i]), ab_w_in[j].astype(BF16), two(a_q_norm[j]),
                two(a_k_norm[j]), cos_t, sin_t, seg, seq)
            r3 = lambda a: a.reshape(bsz, seq, a.shape[1])
            ya = _gqa(r3(aq), r3(akd), r3(avd))
            yb = _na(r3(bq), r3(bk), r3(bv), _na_bias(b_rpb[j], rows))
            mix = (ya.reshape(n, A_Q), yb.reshape(n, B_W), ab_wo, j)
        else:
            zc, cb, zd = _cd_in(xf, row_vec(mix_norm[i]), cd_w_in[j].astype(BF16))
            r3 = lambda a: a.reshape(bsz, seq, a.shape[1])
            yc, yd = _conv(r3(zc), r3(cb), r3(zd), c_conv_w[j].astype(F32), d_conv_w[j].astype(F32),
                           row_vec(d_norm_g[j]), row_vec(d_norm_b[j]))
            mix = (yc.reshape(n, C_WIDTH), yd.reshape(n, D_WIDTH), cd_wo, j)
        xf = ffn(xf, i, 1, mix)
    return xf.reshape(bsz, seq, d)
```

```python
import functools

import jax
import jax.numpy as jnp
import numpy as np
from jax import lax
from jax.experimental import pallas as pl
from jax.experimental.pallas import tpu as pltpu

D_MODEL = 1024
DEPTH = 4
HEAD_DIM = 64
A_HEADS = 8
A_KV_HEADS = 2
B_HEADS = 8
C_WIDTH = 512
D_WIDTH = 512
C_CONV = 3
D_CONV = 31
D_FF = 2816
GRID_W = 64
NA_ROWS = 8
NA_COLS = 16
ROPE_THETA = 10000.0
EPS = 1e-6
NEG_INF = -1e30
A_Q = A_HEADS * HEAD_DIM
A_KV = A_KV_HEADS * HEAD_DIM
B_W = B_HEADS * HEAD_DIM

LANES = 128
SUBLANES = 8
VMEM_LIMIT_BYTES = 56 * 1024 * 1024
TOKEN_TILE = 512
FFN_TILE = 1024
FFN_SUB_TILE = 512
FF_CHUNK = 256
Q_TILE = 2048
Q_CHAIN = 512
NA_ROWS_PER_STEP = 8
CONV_ROWS = 128
D_HALO = 16

F32 = jnp.float32
BF16 = jnp.bfloat16
Q_SCALE = HEAD_DIM ** -0.5 * float(np.log2(np.e))


def _params(*sem):
    return pltpu.CompilerParams(dimension_semantics=sem, vmem_limit_bytes=VMEM_LIMIT_BYTES)


def _resident(shape):
    zeros = (0,) * len(shape)
    return pl.BlockSpec(shape, lambda *_: zeros, pipeline_mode=pl.Buffered(1))


def _resident_slab(stacked, index):
    tail = stacked.shape[len(index):]
    at = tuple(index) + (0,) * len(tail)
    return pl.BlockSpec((None,) * len(index) + tail, lambda *_: at, pipeline_mode=pl.Buffered(1))


def _rms_rows(x, gain):
    ms = jnp.mean(x * x, axis=-1, keepdims=True)
    return x * lax.rsqrt(ms + EPS) * gain


def _ffn_kernel(*refs, final_norm, mixed):
    if mixed:
        x_ref, ya_ref, yb_ref, wo_ref, nw_ref, wg_ref, wu_ref, wd_ref, fw_ref, o_ref, h_ref = refs
        half = ya_ref.shape[1]
    else:
        x_ref, nw_ref, wg_ref, wu_ref, wd_ref, fw_ref, o_ref, h_ref = refs
    nch = D_FF // FF_CHUNK
    subs = [slice(t * FFN_SUB_TILE, (t + 1) * FFN_SUB_TILE) for t in range(FFN_TILE // FFN_SUB_TILE)]
    for rows in subs:
        x1 = x_ref[rows, :]
        if mixed:
            x1 = (x1 + jnp.dot(ya_ref[rows, :], wo_ref[:half], preferred_element_type=F32)
                  + jnp.dot(yb_ref[rows, :], wo_ref[half:], preferred_element_type=F32))
        h_ref[rows, :] = _rms_rows(x1, nw_ref[...]).astype(BF16)
        o_ref[rows, :] = x1
    for rows in subs:
        for c in range(nch):
            cols = slice(c * FF_CHUNK, (c + 1) * FF_CHUNK)
            h = h_ref[rows, :]
            g = jnp.dot(h, wg_ref[:, cols], preferred_element_type=F32)
            u = jnp.dot(h, wu_ref[:, cols], preferred_element_type=F32)
            a = (g * jax.nn.sigmoid(g) * u).astype(BF16)
            o_ref[rows, :] += 0.5 * jnp.dot(a, wd_ref[cols, :], preferred_element_type=F32)
        if final_norm:
            o_ref[rows, :] = _rms_rows(o_ref[rows, :], fw_ref[...])


def _ffn(x, nw, wg, wu, wd, layer, fw, final_norm, mix=None):
    n = x.shape[0]
    tile = lambda w: pl.BlockSpec((FFN_TILE, w), lambda i: (i, 0))
    ffn_specs = [_resident((1, D_MODEL)), _resident_slab(wg, layer), _resident_slab(wu, layer),
                 _resident_slab(wd, layer), _resident((1, D_MODEL))]
    if mix is None:
        operands, mix_specs = (x,), [tile(D_MODEL)]
    else:
        ya, yb, wo, j = mix
        operands = (x, ya, yb, wo)
        mix_specs = [tile(D_MODEL), tile(ya.shape[1]), tile(yb.shape[1]), _resident_slab(wo, (j,))]
    return pl.pallas_call(
        functools.partial(_ffn_kernel, final_norm=final_norm, mixed=mix is not None),
        grid=(n // FFN_TILE,),
        in_specs=mix_specs + ffn_specs,
        out_specs=tile(D_MODEL),
        out_shape=jax.ShapeDtypeStruct(x.shape, F32),
        scratch_shapes=[pltpu.VMEM((FFN_TILE, D_MODEL), BF16)],
        compiler_params=_params("parallel"),
        name="ffn_mix" if mix is not None else "ffn",
    )(*operands, nw, wg, wu, wd, fw)


def _swap_pairs(x):
    lane = lax.broadcasted_iota(jnp.int32, x.shape, 1)
    return jnp.where(lane % 2 == 0, pltpu.roll(x, LANES - 1, 1), pltpu.roll(x, 1, 1))


def _head_norm_rope(xg, gain, cos, sin_signed, seg_ones):
    sq = xg * xg
    hi = sq.astype(BF16)
    lo = (sq - hi.astype(F32)).astype(BF16)
    ssum = jnp.dot(jnp.concatenate([hi, lo], axis=1), seg_ones, preferred_element_type=F32)
    xn = xg * lax.rsqrt(ssum * (1.0 / HEAD_DIM) + EPS) * gain
    return xn * cos + _swap_pairs(xn) * sin_signed


def _dup_halves(x):
    lane = lax.broadcasted_iota(jnp.int32, x.shape, 1)
    rolled = pltpu.roll(x, HEAD_DIM, 1)
    low = lane < HEAD_DIM
    return jnp.where(low, x, rolled), jnp.where(low, rolled, x)


def _ab_in_kernel(x_ref, nw_ref, w_ref, qg_ref, kg_ref, cos_ref, sin_ref, seg_ref,
                  aq_ref, ak_ref, av_ref, bq_ref, bk_ref, bv_ref, h_ref):
    h_ref[...] = _rms_rows(x_ref[...], nw_ref[...]).astype(BF16)
    h = h_ref[...]
    cos = cos_ref[...]
    sin = sin_ref[...]
    seg = seg_ref[...]

    def proj(lo, width):
        return jnp.dot(h, w_ref[:, lo:lo + width], preferred_element_type=F32)

    aq = proj(0, A_Q)
    for j in range(A_Q // LANES):
        q = _head_norm_rope(aq[:, j * LANES:(j + 1) * LANES], qg_ref[...], cos, sin, seg)
        aq_ref[:, j * LANES:(j + 1) * LANES] = (q * Q_SCALE).astype(BF16)
    akv = proj(A_Q, 2 * A_KV)
    k0, k1 = _dup_halves(_head_norm_rope(akv[:, :A_KV], kg_ref[...], cos, sin, seg))
    ak_ref[:, :LANES] = k0.astype(BF16)
    ak_ref[:, LANES:] = k1.astype(BF16)
    v0, v1 = _dup_halves(akv[:, A_KV:])
    av_ref[:, :LANES] = v0.astype(BF16)
    av_ref[:, LANES:] = v1.astype(BF16)
    base = A_Q + 2 * A_KV
    bq_ref[...] = (proj(base, B_W) * Q_SCALE).astype(BF16)
    bk_ref[...] = proj(base + B_W, B_W).astype(BF16)
    bv_ref[...] = proj(base + 2 * B_W, B_W).astype(BF16)


def _ab_in(x, nw, w_in, qg, kg, cos_t, sin_t, seg, seq):
    n = x.shape[0]
    ab_in = w_in.shape[1]
    tiles_per_seq = seq // TOKEN_TILE
    tile = lambda w: pl.BlockSpec((TOKEN_TILE, w), lambda i: (i, 0))
    rope = pl.BlockSpec((TOKEN_TILE, LANES), lambda i: (i % tiles_per_seq, 0))
    widths = (A_Q, 2 * LANES, 2 * LANES, B_W, B_W, B_W)
    return pl.pallas_call(
        _ab_in_kernel,
        grid=(n // TOKEN_TILE,),
        in_specs=[tile(D_MODEL), _resident((1, D_MODEL)), _resident((D_MODEL, ab_in)),
                  _resident((1, LANES)), _resident((1, LANES)), rope, rope,
                  _resident((2 * LANES, LANES))],
        out_specs=[tile(w) for w in widths],
        out_shape=[jax.ShapeDtypeStruct((n, w), BF16) for w in widths],
        scratch_shapes=[pltpu.VMEM((TOKEN_TILE, D_MODEL), BF16)],
        compiler_params=_params("parallel"),
        name="ab_in",
    )(x, nw, w_in, qg, kg, cos_t, sin_t, seg)


def _stack_heads(qp):
    lane = lax.broadcasted_iota(jnp.int32, qp.shape, 1)
    zero = jnp.zeros_like(qp)
    return jnp.concatenate([jnp.where(lane < HEAD_DIM, qp, zero),
                            jnp.where(lane < HEAD_DIM, zero, qp)], axis=0)


def _unstack_heads(o, rows):
    lane = lax.broadcasted_iota(jnp.int32, (rows, LANES), 1)
    return jnp.where(lane < HEAD_DIM, o[:rows], o[rows:])


def _softmax_pv(s, v_ones):
    m = jnp.max(s, axis=-1, keepdims=True)
    p = jnp.exp2(s - m).astype(BF16)
    pv = jnp.dot(p, v_ones, preferred_element_type=F32)
    return pv / pltpu.roll(pv, HEAD_DIM, 1)


_NT = (((1,), (1,)), ((), ()))


def _gqa_kernel(q_ref, k_ref, v_ref, o_ref):
    k = k_ref[0]
    v = v_ref[0]
    vlane = lax.broadcasted_iota(jnp.int32, v.shape, 1)
    one = jnp.ones_like(v)
    v_ones = (jnp.where(vlane < HEAD_DIM, v, one), jnp.where(vlane < HEAD_DIM, one, v))
    lane = lax.broadcasted_iota(jnp.int32, (Q_CHAIN, LANES), 1)
    masks = (lane < HEAD_DIM, lane >= HEAD_DIM)
    chains = [(blk, h) for blk in range(Q_TILE // Q_CHAIN) for h in range(A_HEADS // A_KV_HEADS)]

    def block(c):
        blk, h = chains[c]
        return slice(blk * Q_CHAIN, (blk + 1) * Q_CHAIN), slice((h // 2) * LANES, (h // 2 + 1) * LANES)

    def scores(c):
        rows, cols = block(c)
        qp = q_ref[0, rows, cols]
        qh = jnp.where(masks[chains[c][1] % 2], qp, jnp.zeros_like(qp))
        return lax.dot_general(qh, k, _NT, preferred_element_type=F32)

    s_next = scores(0)
    outs = []
    for c in range(len(chains)):
        s_cur = s_next
        if c + 1 < len(chains):
            s_next = scores(c + 1)
        outs.append(_softmax_pv(s_cur, v_ones[chains[c][1] % 2]))
        if c % 2 == 1:
            rows, cols = block(c)
            o_ref[0, rows, cols] = jnp.where(lane < HEAD_DIM, outs[c - 1], outs[c]).astype(BF16)


def _gqa(aq, akd, avd):
    b, s, _ = aq.shape
    qspec = pl.BlockSpec((1, Q_TILE, 2 * LANES), lambda bi, g, qi: (bi, qi, g))
    kvspec = pl.BlockSpec((1, s, LANES), lambda bi, g, qi: (bi, 0, g))
    return pl.pallas_call(
        _gqa_kernel,
        grid=(b, A_KV_HEADS, s // Q_TILE),
        in_specs=[qspec, kvspec, kvspec],
        out_specs=qspec,
        out_shape=jax.ShapeDtypeStruct(aq.shape, BF16),
        compiler_params=_params("parallel", "parallel", "arbitrary"),
        name="gqa",
    )(aq, akd, avd)


def _na_kernel(q_ref, k_ref, v_ref, bias_ref, o_ref, *, rows):
    band = NA_ROWS * GRID_W
    pairs = B_W // LANES
    contract_rows = (((0,), (0,)), ((), ()))

    def row_group(i, carry):
        chains = []
        for rr in range(NA_ROWS_PER_STEP):
            r = i * NA_ROWS_PER_STEP + rr
            rs = jnp.clip(r - NA_ROWS // 2, 0, rows - NA_ROWS)
            q0 = pl.multiple_of(r * GRID_W, GRID_W)
            k0 = pl.multiple_of(rs * GRID_W, GRID_W)
            chains += [(r - rs, q0, k0, j) for j in range(pairs)]

        def scores(c):
            d, q0, k0, j = chains[c]
            cols = slice(j * LANES, (j + 1) * LANES)
            qs = _stack_heads(q_ref[0, pl.ds(q0, GRID_W), cols])
            kb = k_ref[0, pl.ds(k0, band), cols]
            return lax.dot_general(kb, qs, _NT, preferred_element_type=F32) + bias_ref[d, j]

        s_next = scores(0)
        for c in range(len(chains)):
            s_cur = s_next
            if c + 1 < len(chains):
                s_next = scores(c + 1)
            d, q0, k0, j = chains[c]
            cols = slice(j * LANES, (j + 1) * LANES)
            m = jnp.max(s_cur, axis=0, keepdims=True)
            p = jnp.exp2(s_cur - m)
            l = jnp.sum(p, axis=0, keepdims=True)
            pn = (p * (1.0 / l)).astype(BF16)
            o = lax.dot_general(pn, v_ref[0, pl.ds(k0, band), cols], contract_rows, preferred_element_type=F32)
            o_ref[0, pl.ds(q0, GRID_W), cols] = _unstack_heads(o, GRID_W).astype(BF16)
        return carry

    lax.fori_loop(0, rows // NA_ROWS_PER_STEP, row_group, 0)


def _na(bq, bk, bv, bias):
    b, s, w = bq.shape
    rows = s // GRID_W
    spec = pl.BlockSpec((1, s, w), lambda bi: (bi, 0, 0))
    return pl.pallas_call(
        functools.partial(_na_kernel, rows=rows),
        grid=(b,),
        in_specs=[spec, spec, spec, _resident(bias.shape)],
        out_specs=spec,
        out_shape=jax.ShapeDtypeStruct(bq.shape, BF16),
        compiler_params=_params("parallel"),
        name="na",
    )(bq, bk, bv, bias)


def _na_bias(rpb, rows):
    wr = min(NA_ROWS, rows)
    qc = np.arange(GRID_W)[:, None]
    kc = np.arange(GRID_W)[None, :]
    col_off = np.clip(kc - qc, -(NA_COLS - 1), NA_COLS - 1) + NA_COLS - 1
    wcs = np.clip(qc - NA_COLS // 2, 0, GRID_W - NA_COLS)
    valid = (kc >= wcs) & (kc < wcs + NA_COLS)
    onehot = (col_off[None] == np.arange(2 * NA_COLS - 1)[:, None, None]).astype(np.float32)
    by_col = jnp.einsum("hrc,cqk->hrqk", rpb.astype(F32), jnp.asarray(onehot),
                        precision=lax.Precision.HIGHEST)
    by_col = by_col.reshape(B_HEADS // 2, 2, 2 * NA_ROWS - 1, GRID_W, GRID_W).transpose(0, 2, 4, 1, 3)
    by_col = by_col * float(np.log2(np.e))
    t = jnp.stack([by_col[:, NA_ROWS - 1 - d:NA_ROWS - 1 - d + wr] for d in range(wr)])
    t = jnp.where(valid.T[None, None, None, :, None, :], t, NEG_INF)
    return t.reshape(wr, B_HEADS // 2, wr * GRID_W, 2 * GRID_W)


def _cd_in_kernel(x_ref, nw_ref, w_ref, zc_ref, cb_ref, zd_ref, h_ref):
    h_ref[...] = _rms_rows(x_ref[...], nw_ref[...]).astype(BF16)
    h = h_ref[...]

    def proj(i):
        return jnp.dot(h, w_ref[:, i * C_WIDTH:(i + 1) * C_WIDTH], preferred_element_type=F32)

    zc_ref[...] = (proj(2) * proj(0)).astype(BF16)
    cb_ref[...] = proj(1).astype(BF16)
    zd_ref[...] = (proj(3) * jax.nn.sigmoid(proj(4))).astype(BF16)


def _cd_in(x, nw, w_in):
    n = x.shape[0]
    tile = lambda w: pl.BlockSpec((TOKEN_TILE, w), lambda i: (i, 0))
    return pl.pallas_call(
        _cd_in_kernel,
        grid=(n // TOKEN_TILE,),
        in_specs=[tile(D_MODEL), _resident((1, D_MODEL)), _resident(w_in.shape)],
        out_specs=[tile(C_WIDTH)] * 3,
        out_shape=[jax.ShapeDtypeStruct((n, C_WIDTH), BF16)] * 3,
        scratch_shapes=[pltpu.VMEM((TOKEN_TILE, D_MODEL), BF16)],
        compiler_params=_params("parallel"),
        name="cd_in",
    )(x, nw, w_in)


def _conv_kernel(zc_ref, cb_ref, zd_ref, wc_ref, wd_ref, g_ref, b_ref, yc_ref, yd_ref,
                 pad_ref, raw_ref, *, seq):
    halo = jnp.zeros((D_HALO, C_WIDTH), F32)
    nchunks = seq // CONV_ROWS
    window = CONV_ROWS + 2 * D_HALO

    def conv(src_ref, w_ref, taps, emit):
        pad_ref[:D_HALO] = halo
        pad_ref[D_HALO + seq:] = halo
        pad_ref[D_HALO:D_HALO + seq] = src_ref[0].astype(F32)
        first = D_HALO - taps // 2
        for c in range(C_WIDTH // LANES):
            cols = slice(c * LANES, (c + 1) * LANES)
            w = w_ref[:, cols]

            def chunk(i, carry):
                r0 = pl.multiple_of(i * CONV_ROWS, CONV_ROWS)
                win = pad_ref[pl.ds(r0, window), cols]
                acc = jnp.zeros((CONV_ROWS, LANES), F32)
                for sh in range(SUBLANES):
                    offs = [first + k for k in range(taps) if (first + k) % SUBLANES == sh]
                    if not offs:
                        continue
                    shifted = win if sh == 0 else pltpu.roll(win, window - sh, 0)
                    for off in offs:
                        k = off - first
                        base = off - sh
                        acc = acc + w[k:k + 1, :] * shifted[base:base + CONV_ROWS]
                emit(r0, cols, acc)
                return carry

            lax.fori_loop(0, nchunks, chunk, 0, unroll=4)

    def emit_c(r0, cols, acc):
        yc_ref[0, pl.ds(r0, CONV_ROWS), cols] = (
            cb_ref[0, pl.ds(r0, CONV_ROWS), cols].astype(F32) * acc).astype(BF16)

    def emit_d(r0, cols, acc):
        raw_ref[pl.ds(r0, CONV_ROWS), cols] = acc

    conv(zc_ref, wc_ref, C_CONV, emit_c)
    conv(zd_ref, wd_ref, D_CONV, emit_d)

    def norm(i, carry):
        r0 = pl.multiple_of(i * CONV_ROWS, CONV_ROWS)
        y = raw_ref[pl.ds(r0, CONV_ROWS), :]
        mu = jnp.mean(y, axis=-1, keepdims=True)
        var = jnp.mean(jnp.square(y - mu), axis=-1, keepdims=True)
        z = (y - mu) * lax.rsqrt(var + EPS) * g_ref[...] + b_ref[...]
        yd_ref[0, pl.ds(r0, CONV_ROWS), :] = (z * jax.nn.sigmoid(z)).astype(BF16)
        return carry

    lax.fori_loop(0, nchunks, norm, 0, unroll=4)


def _conv(zc, cb, zd, wc, wd, g, b):
    bsz, seq, w = zc.shape
    padded = seq + 2 * D_HALO
    spec = pl.BlockSpec((1, seq, w), lambda bi: (bi, 0, 0))
    return pl.pallas_call(
        functools.partial(_conv_kernel, seq=seq),
        grid=(bsz,),
        in_specs=[spec, spec, spec, _resident(wc.shape), _resident(wd.shape),
                  _resident((1, w)), _resident((1, w))],
        out_specs=[spec, spec],
        out_shape=[jax.ShapeDtypeStruct(zc.shape, BF16)] * 2,
        scratch_shapes=[pltpu.VMEM((padded, w), F32),
                        pltpu.VMEM((seq, w), F32)],
        compiler_params=_params("parallel"),
        name="conv",
    )(zc, cb, zd, wc, wd, g, b)


def _rope_tables(seq):
    t = jnp.arange(seq)
    row = (t // GRID_W).astype(F32)
    col = (t % GRID_W).astype(F32)
    half = HEAD_DIM // 2
    freqs = ROPE_THETA ** (-jnp.arange(0, half, 2, dtype=F32) / half)
    ang = jnp.concatenate([row[:, None] * freqs, col[:, None] * freqs], axis=-1)
    cos = jnp.repeat(jnp.cos(ang), 2, axis=-1)
    sin = jnp.repeat(jnp.sin(ang), 2, axis=-1)
    sign = jnp.tile(jnp.array([-1.0, 1.0], F32), half)
    return jnp.tile(cos, (1, 2)), jnp.tile(sin * sign, (1, 2))


def kernel(x, ffn_norm, mix_norm, ffn_w_gate, ffn_w_up, ffn_w_down, ab_w_in, ab_w_out,
           a_q_norm, a_k_norm, b_rpb, cd_w_in, cd_w_out, c_conv_w, d_conv_w,
           d_norm_g, d_norm_b, final_norm):
    bsz, seq, d = x.shape
    assert d == D_MODEL and seq % TOKEN_TILE == 0 and seq % GRID_W == 0
    n = bsz * seq
    rows = seq // GRID_W
    cos_t, sin_t = _rope_tables(seq)
    seg = jnp.asarray(np.tile(np.kron(np.eye(LANES // HEAD_DIM), np.ones((HEAD_DIM, HEAD_DIM))), (2, 1)), BF16)
    row_vec = lambda v: v.reshape(1, -1).astype(F32)

    wg, wu, wd = (w.astype(BF16) for w in (ffn_w_gate, ffn_w_up, ffn_w_down))
    ab_wo, cd_wo = ab_w_out.astype(BF16), cd_w_out.astype(BF16)

    def ffn(xf, i, k, mix=None):
        last = i == DEPTH - 1 and k == 1
        return _ffn(xf, row_vec(ffn_norm[i, k]), wg, wu, wd, (i, k),
                    row_vec(final_norm), final_norm=last, mix=mix)

    xf = x.reshape(n, d)
    for i in range(DEPTH):
        xf = ffn(xf, i, 0)
        j = i // 2
        if i % 2 == 0:
            two = lambda g: row_vec(jnp.tile(g, LANES // HEAD_DIM))
            aq, akd, avd, bq, bk, bv = _ab_in(
                xf, row_vec(mix_norm[i]), ab_w_in[j].astype(BF16), two(a_q_norm[j]),
                two(a_k_norm[j]), cos_t, sin_t, seg, seq)
            r3 = lambda a: a.reshape(bsz, seq, a.shape[1])
            ya = _gqa(r3(aq), r3(akd), r3(avd))
            yb = _na(r3(bq), r3(bk), r3(bv), _na_bias(b_rpb[j], rows))
            mix = (ya.reshape(n, A_Q), yb.reshape(n, B_W), ab_wo, j)
        else:
            zc, cb, zd = _cd_in(xf, row_vec(mix_norm[i]), cd_w_in[j].astype(BF16))
            r3 = lambda a: a.reshape(bsz, seq, a.shape[1])
            yc, yd = _conv(r3(zc), r3(cb), r3(zd), c_conv_w[j].astype(F32), d_conv_w[j].astype(F32),
                           row_vec(d_norm_g[j]), row_vec(d_norm_b[j]))
            mix = (yc.reshape(n, C_WIDTH), yd.reshape(n, D_WIDTH), cd_wo, j)
        xf = ffn(xf, i, 1, mix)
    return xf.reshape(bsz, seq, d)
```
